```python
import math
import jax, jax.numpy as jnp
from jax import lax
import numpy as np

D_MODEL = 1024
BATCH = 4
SEQ = 8192
DEPTH = 2

CTX_LEN = 256
GRID_W = 64
SGU_CHUNK = 128
SGU_ROWS_PER_CHUNK = SGU_CHUNK // GRID_W
SGU_WIDTH = D_MODEL // 2
SGU_GROUPS = 4
SGU_GROUP_DIM = SGU_WIDTH // SGU_GROUPS
GDN_HEADS = 4
GDN_HEAD_DIM = 128
GDN_WIDTH = GDN_HEADS * GDN_HEAD_DIM
GDN_CHUNK = 64
CONV_K = 5
D_FF = 3584
N_EXPERTS = 8
TOP_K = 2
EXPERT_FF = 3584
N_DENSE = (DEPTH + 1) // 2
N_MOE = DEPTH // 2
ALPHA = (2.0 * DEPTH) ** 0.25
BETA = (8.0 * DEPTH) ** -0.25
LN_EPS = 1e-6
IN_SIZES = (SGU_WIDTH, SGU_WIDTH, 3 * GDN_WIDTH, GDN_WIDTH, GDN_HEADS, GDN_HEADS, GDN_HEADS, GDN_HEADS, D_MODEL, D_MODEL)
N_IN = 2 * SGU_WIDTH + 4 * GDN_WIDTH + 4 * GDN_HEADS + 2 * D_MODEL

kernel_name = 'hybrid_sgu_gdn_moe_prefix_block'


def layer_norm(x, g=None, b=None):
    xf = x.astype(jnp.float32)
    mu = jnp.mean(xf, axis=-1, keepdims=True)
    var = jnp.mean(jnp.square(xf - mu), axis=-1, keepdims=True)
    y = (xf - mu) * lax.rsqrt(var + LN_EPS)
    if g is not None:
        y = y * g.astype(jnp.float32) + b.astype(jnp.float32)
    return y.astype(x.dtype)


def modulate(x, shift, scale):
    return layer_norm(x) * (1.0 + scale) + shift


def split_in(p):
    idx = [int(i) for i in np.cumsum(IN_SIZES)[:-1]]
    return jnp.split(p, idx, axis=-1)


def l2norm(x):
    xf = x.astype(jnp.float32)
    return xf * lax.rsqrt(jnp.sum(xf * xf, axis=-1, keepdims=True) + LN_EPS)


def short_conv(x, w):
    pad = w.shape[0] // 2
    return lax.conv_general_dilated(x, w[:, None, :].astype(x.dtype), window_strides=(1,),
                                    padding=[(pad, pad)], dimension_numbers=('NWC', 'WIO', 'NWC'),
                                    feature_group_count=x.shape[-1])


def sgu(u, v, ln_g, ln_b, w_s, b_s, n_chunks):
    bsz = u.shape[0]
    vn = layer_norm(v, ln_g, ln_b).reshape(bsz, n_chunks, SGU_CHUNK, SGU_GROUPS, SGU_GROUP_DIM)
    mixed = jnp.einsum('gpq,bnqgc->bnpgc', w_s, vn) + b_s.T[:, :, None]
    return u * mixed.reshape(u.shape)


def gated_delta_chunked(q, k, v, log_g, beta, s0):
    bsz, t, h, dk = q.shape
    dv = v.shape[-1]
    n = t // GDN_CHUNK
    f32 = jnp.float32

    def blocks(a):
        a = a.astype(f32).reshape((bsz, n, GDN_CHUNK, h) + a.shape[3:])
        return jnp.moveaxis(a, 3, 1)

    qb, kb, vb = blocks(q), blocks(k), blocks(v)
    gb, bb = blocks(log_g), blocks(beta)
    gam = jnp.cumsum(gb, axis=-1)
    pos = jnp.arange(GDN_CHUNK)
    incl = pos[:, None] >= pos[None, :]
    strict = pos[:, None] > pos[None, :]
    decay = jnp.exp(jnp.where(incl, gam[..., :, None] - gam[..., None, :], -jnp.inf))
    kk = jnp.einsum('bhncd,bhnsd->bhncs', kb, kb)
    lower = jnp.where(strict, bb[..., :, None] * kk * decay, 0.0) + jnp.eye(GDN_CHUNK, dtype=f32)
    rhs = jnp.concatenate([bb[..., None] * vb, (bb * jnp.exp(gam))[..., None] * kb], axis=-1)
    sol = lax.linalg.triangular_solve(lower, rhs, left_side=True, lower=True, unit_diagonal=True)
    u, wk = sol[..., :dv], sol[..., dv:]
    qk = jnp.einsum('bhncd,bhnsd->bhncs', qb, kb) * decay
    qg = qb * jnp.exp(gam)[..., None]
    kd = kb * jnp.exp(gam[..., -1:] - gam)[..., None]
    gc = jnp.exp(gam[..., -1])
    xs = tuple(jnp.moveaxis(a, 2, 0) for a in (u, wk, qk, qg, kd, gc))

    def step(s, inp):
        u_n, wk_n, qk_n, qg_n, kd_n, gc_n = inp
        w = u_n - jnp.einsum('bhcd,bhdv->bhcv', wk_n, s)
        o = jnp.einsum('bhcd,bhdv->bhcv', qg_n, s) + jnp.einsum('bhcs,bhsv->bhcv', qk_n, w)
        s = gc_n[..., None, None] * s + jnp.einsum('bhcd,bhcv->bhdv', kd_n, w)
        return s, o

    s_fin, o = lax.scan(step, s0.astype(f32), xs)
    o = jnp.transpose(o, (1, 0, 3, 2, 4)).reshape(bsz, t, h, dv)
    return o.astype(v.dtype), s_fin


def token_mixer(h_lat, h_ctx, n_chunks_lat, n_chunks_ctx, w_in, sgu_ln_g, sgu_ln_b, sgu_w, sgu_b,
                conv_w, a_log, dt_bias, norm_w, w_pa, w_pb, w_o, need_ctx):
    f32 = jnp.float32
    u_l, v_l, qkv_l, z_l, af_l, bf_l, ab_l, bb_l, ga_l, gb_l = split_in(h_lat @ w_in)
    u_c, v_c, qkv_c, z_c, af_c, bf_c, ab_c, bb_c, ga_c, gb_c = split_in(h_ctx @ w_in)

    def qkv_heads(qkv):
        qkv = jax.nn.silu(short_conv(qkv, conv_w))
        q, k, v = jnp.split(qkv, 3, axis=-1)
        shape = qkv.shape[:2] + (GDN_HEADS, GDN_HEAD_DIM)
        q = l2norm(q.reshape(shape)) * (GDN_HEAD_DIM ** -0.5)
        k = l2norm(k.reshape(shape))
        return q, k, v.reshape(shape)

    def gates(a, b, d):
        lg = -jnp.exp(a_log[d].astype(f32)) * jax.nn.softplus(a.astype(f32) + dt_bias[d].astype(f32))
        return lg, jax.nn.sigmoid(b.astype(f32))

    def flip(t):
        return jnp.flip(t, axis=1)

    def gated_out(o, z):
        of = o.astype(f32)
        of = of * lax.rsqrt(jnp.mean(of * of, axis=-1, keepdims=True) + LN_EPS) * norm_w.astype(f32)
        return (of.reshape(z.shape) * jax.nn.silu(z.astype(f32))).astype(z.dtype)

    def merge(o_a, o_b, g_a, g_b):
        y = jax.nn.sigmoid(g_a) * (o_a @ w_pa) + jax.nn.sigmoid(g_b) * (o_b @ w_pb)
        return y @ w_o

    qc, kc, vc = qkv_heads(qkv_c)
    ql, kl, vl = qkv_heads(qkv_l)
    gfc, bfc = gates(af_c, bf_c, 0)
    gbc, bbc = gates(ab_c, bb_c, 1)
    gfl, bfl = gates(af_l, bf_l, 0)
    gbl, bbl = gates(ab_l, bb_l, 1)
    s_zero = jnp.zeros((h_ctx.shape[0], GDN_HEADS, GDN_HEAD_DIM, GDN_HEAD_DIM), f32)
    o_cf, s_cf = gated_delta_chunked(qc, kc, vc, gfc, bfc, s_zero)
    o_cb, s_cb = gated_delta_chunked(flip(qc), flip(kc), flip(vc), flip(gbc), flip(bbc), s_zero)
    o_lf, _ = gated_delta_chunked(ql, kl, vl, gfl, bfl, s_cf)
    o_lb, _ = gated_delta_chunked(flip(ql), flip(kl), flip(vl), flip(gbl), flip(bbl), s_cb)

    oa_l = sgu(jax.nn.gelu(u_l, approximate=False), jax.nn.gelu(v_l, approximate=False),
               sgu_ln_g, sgu_ln_b, sgu_w, sgu_b, n_chunks_lat)
    y_lat = merge(oa_l, gated_out(o_lf + flip(o_lb), z_l), ga_l, gb_l)
    y_ctx = None
    if need_ctx:
        oa_c = sgu(jax.nn.gelu(u_c, approximate=False), jax.nn.gelu(v_c, approximate=False),
                   sgu_ln_g, sgu_ln_b, sgu_w, sgu_b, n_chunks_ctx)
        y_ctx = merge(oa_c, gated_out(o_cf + flip(o_cb), z_c), ga_c, gb_c)
    return y_lat, y_ctx


def swiglu(h, w1, w3, w2):
    return (jax.nn.silu(h @ w1) * (h @ w3)) @ w2


def moe_ffn(h, router, w1, w3, w2):
    logits = jnp.einsum('btd,de->bte', h, router).astype(jnp.float32)
    top_val, top_idx = lax.top_k(logits, TOP_K)
    top_w = jax.nn.softmax(top_val, axis=-1)
    gate = jnp.sum(jax.nn.one_hot(top_idx, N_EXPERTS, dtype=jnp.float32) * top_w[..., None], axis=-2)
    gate = gate.astype(h.dtype)
    y = jnp.zeros_like(h)
    for e in range(N_EXPERTS):
        y = y + gate[..., e:e + 1] * swiglu(h, w1[e], w3[e], w2[e])
    return y


def setup_inputs(seed: int = 0) -> dict:
    key = jax.random.key(seed)
    ks = iter(jax.random.split(key, 40))
    nrm = lambda shape: jax.random.normal(next(ks), shape, jnp.float32)
    dt = jnp.exp(jax.random.uniform(next(ks), (DEPTH, 2, GDN_HEADS), jnp.float32,
                                    math.log(1e-3), math.log(1e-1)))
    return {
        'x': nrm((BATCH, SEQ, D_MODEL)),
        'c': nrm((BATCH, D_MODEL)),
        'ctx': nrm((BATCH, CTX_LEN, D_MODEL)),
        'c_ctx': nrm((D_MODEL,)),
        'w_ada': nrm((DEPTH, D_MODEL, 6 * D_MODEL)) * D_MODEL ** -0.5,
        'b_ada': 0.02 * nrm((DEPTH, 6 * D_MODEL)),
        'w_in': nrm((DEPTH, D_MODEL, N_IN)) * D_MODEL ** -0.5,
        'sgu_ln_g': 1.0 + 0.02 * nrm((DEPTH, SGU_WIDTH)),
        'sgu_ln_b': 0.02 * nrm((DEPTH, SGU_WIDTH)),
        'sgu_w': nrm((DEPTH, SGU_GROUPS, SGU_CHUNK, SGU_CHUNK)) * SGU_CHUNK ** -0.5,
        'sgu_b': 1.0 + 0.02 * nrm((DEPTH, SGU_GROUPS, SGU_CHUNK)),
        'conv_w': nrm((DEPTH, CONV_K, 3 * GDN_WIDTH)) * CONV_K ** -0.5,
        'a_log': jnp.log(jax.random.uniform(next(ks), (DEPTH, 2, GDN_HEADS), jnp.float32, 1.0, 16.0)),
        'dt_bias': dt + jnp.log(-jnp.expm1(-dt)),
        'gdn_norm_w': 1.0 + 0.02 * nrm((DEPTH, GDN_HEAD_DIM)),
        'w_pa': nrm((DEPTH, SGU_WIDTH, D_MODEL)) * (SGU_WIDTH ** -0.5 * BETA),
        'w_pb': nrm((DEPTH, GDN_WIDTH, D_MODEL)) * (GDN_WIDTH ** -0.5 * BETA),
        'w_o': nrm((DEPTH, D_MODEL, D_MODEL)) * (D_MODEL ** -0.5 * BETA),
        'ln1_g': 1.0 + 0.02 * nrm((DEPTH, D_MODEL)),
        'ln1_b': 0.02 * nrm((DEPTH, D_MODEL)),
        'ln2_g': 1.0 + 0.02 * nrm((DEPTH, D_MODEL)),
        'ln2_b': 0.02 * nrm((DEPTH, D_MODEL)),
        'ffn_w1': nrm((N_DENSE, D_MODEL, D_FF)) * D_MODEL ** -0.5,
        'ffn_w3': nrm((N_DENSE, D_MODEL, D_FF)) * D_MODEL ** -0.5,
        'ffn_w2': nrm((N_DENSE, D_FF, D_MODEL)) * (D_FF ** -0.5 * BETA),
        'moe_router': nrm((N_MOE, D_MODEL, N_EXPERTS)) * D_MODEL ** -0.5,
        'moe_w1': nrm((N_MOE, N_EXPERTS, D_MODEL, EXPERT_FF)) * D_MODEL ** -0.5,
        'moe_w3': nrm((N_MOE, N_EXPERTS, D_MODEL, EXPERT_FF)) * D_MODEL ** -0.5,
        'moe_w2': nrm((N_MOE, N_EXPERTS, EXPERT_FF, D_MODEL)) * (EXPERT_FF ** -0.5 * BETA),
    }


def reference(x, c, ctx, c_ctx, w_ada, b_ada, w_in, sgu_ln_g, sgu_ln_b, sgu_w, sgu_b, conv_w, a_log,
              dt_bias, gdn_norm_w, w_pa, w_pb, w_o, ln1_g, ln1_b, ln2_g, ln2_b, ffn_w1, ffn_w3, ffn_w2,
              moe_router, moe_w1, moe_w3, moe_w2):
    rows = x.shape[1] // GRID_W
    n_chunks_lat = rows // SGU_ROWS_PER_CHUNK
    n_chunks_ctx = ctx.shape[1] // SGU_CHUNK
    silu_c = jax.nn.silu(c)
    silu_cc = jax.nn.silu(c_ctx)
    x_lat, x_ctx = x, ctx
    for l in range(DEPTH):
        need_ctx = l < DEPTH - 1
        m_lat = jnp.split((silu_c @ w_ada[l] + b_ada[l])[:, None, :], 6, axis=-1)
        m_ctx = jnp.split(silu_cc @ w_ada[l] + b_ada[l], 6, axis=-1)
        h_lat = modulate(x_lat, m_lat[0], m_lat[1])
        h_ctx = modulate(x_ctx, m_ctx[0], m_ctx[1])
        y_lat, y_ctx = token_mixer(h_lat, h_ctx, n_chunks_lat, n_chunks_ctx, w_in[l], sgu_ln_g[l],
                                   sgu_ln_b[l], sgu_w[l], sgu_b[l], conv_w[l], a_log[l], dt_bias[l],
                                   gdn_norm_w[l], w_pa[l], w_pb[l], w_o[l], need_ctx)
        x_lat = layer_norm(ALPHA * x_lat + m_lat[2] * y_lat, ln1_g[l], ln1_b[l])
        if need_ctx:
            x_ctx = layer_norm(ALPHA * x_ctx + m_ctx[2] * y_ctx, ln1_g[l], ln1_b[l])
        h_lat = modulate(x_lat, m_lat[3], m_lat[4])
        if l % 2 == 0:
            f_lat = swiglu(h_lat, ffn_w1[l // 2], ffn_w3[l // 2], ffn_w2[l // 2])
        else:
            f_lat = moe_ffn(h_lat, moe_router[l // 2], moe_w1[l // 2], moe_w3[l // 2], moe_w2[l // 2])
        if need_ctx:
            h_ctx = modulate(x_ctx, m_ctx[3], m_ctx[4])
            if l % 2 == 0:
                f_ctx = swiglu(h_ctx, ffn_w1[l // 2], ffn_w3[l // 2], ffn_w2[l // 2])
            else:
                f_ctx = moe_ffn(h_ctx, moe_router[l // 2], moe_w1[l // 2], moe_w3[l // 2], moe_w2[l // 2])
            x_ctx = layer_norm(ALPHA * x_ctx + m_ctx[5] * f_ctx, ln2_g[l], ln2_b[l])
        x_lat = layer_norm(ALPHA * x_lat + m_lat[5] * f_lat, ln2_g[l], ln2_b[l])
    return x_lat
```

```python
import functools
import math

import jax
import jax.numpy as jnp
from jax import lax
from jax.experimental import pallas as pl
from jax.experimental.pallas import tpu as pltpu

F32 = jnp.float32
BF16 = jnp.bfloat16

LN_EPS = 1e-6
SGU_CHUNK = 128
SGU_GROUPS = 4
GDN_HEADS = 4
GDN_HEAD_DIM = 128
GDN_WIDTH = GDN_HEADS * GDN_HEAD_DIM
GDN_CHUNK = 64
CONV_K = 5
TOP_K = 2

LANES = 128
TOK_BLK = 256
HALO_ROWS = 16
VMEM_LIMIT = 56 * 1024 * 1024


def _cparams(sem):
    return pltpu.CompilerParams(dimension_semantics=sem, vmem_limit_bytes=VMEM_LIMIT)


def _dot(a, b):
    return jnp.dot(a, b, preferred_element_type=F32)


def _dot_nt(a, b):
    return lax.dot_general(a, b, (((1,), (1,)), ((), ())), preferred_element_type=F32)


def _dot_tn(a, b):
    return lax.dot_general(a, b, (((0,), (0,)), ((), ())), preferred_element_type=F32)


def _split(a):
    hi = a.astype(BF16)
    lo = (a - hi.astype(F32)).astype(BF16)
    return hi, lo


def _dot3(a, b):
    ah, al = _split(a)
    bh, bl = _split(b)
    return _dot(ah, bh) + (_dot(ah, bl) + _dot(al, bh))


def _dot2_exact_lhs(a_bf, b):
    bh, bl = _split(b)
    return _dot(a_bf, bh) + _dot(a_bf, bl)


def _dot2_exact_rhs(a, b_bf):
    ah, al = _split(a)
    return _dot(ah, b_bf) + _dot(al, b_bf)


def _norm(x):
    mu = jnp.mean(x, axis=-1, keepdims=True)
    xc = x - mu
    var = jnp.mean(xc * xc, axis=-1, keepdims=True)
    return xc * lax.rsqrt(var + LN_EPS)


def _sigmoid(x):
    return 1.0 / (1.0 + jnp.exp(-x))


def _silu(x):
    return x * _sigmoid(x)


def _gelu(x):
    return 0.5 * x * (1.0 + lax.erf(x * (2.0 ** -0.5)))


def _blk_id(idx, size):
    return jnp.right_shift(idx, int(math.log2(size)))


def _softplus(x):
    return jnp.maximum(x, 0.0) + jnp.log1p(jnp.exp(-jnp.abs(x)))


def _ada_kernel(c_ref, w_ref, b_ref, o_ref):
    o_ref[...] = _dot3(_silu(c_ref[...]), w_ref[...]) + b_ref[...]


def _ada(cond, w, b):
    rows, d = cond.shape
    n = w.shape[1]
    tn = n // 6
    return pl.pallas_call(
        _ada_kernel,
        grid=(n // tn,),
        in_specs=[pl.BlockSpec((rows, d), lambda j: (0, 0)),
                  pl.BlockSpec((d, tn), lambda j: (0, j)),
                  pl.BlockSpec((1, tn), lambda j: (0, j))],
        out_specs=pl.BlockSpec((rows, tn), lambda j: (0, j)),
        out_shape=jax.ShapeDtypeStruct((rows, n), F32),
        compiler_params=_cparams(("arbitrary",)),
        name="ada",
    )(cond, w, b.reshape(1, n))


def _inproj_kernel(*refs, has_prev, alpha, seg):
    if has_prev:
        (x_ref, f_ref, mprev_ref, lng_ref, lnb_ref, mods_ref, w_ref, wg_ref,
         xo_ref, ug_ref, vg_ref, qkv_ref, zs_ref, sga_ref, sgb_ref, gates_ref) = refs
        x = alpha * x_ref[0] + mprev_ref[0, 0, 5:6, :] * f_ref[0].astype(F32)
        x = _norm(x) * lng_ref[...] + lnb_ref[...]
        xo_ref[0] = x
    else:
        (x_ref, mods_ref, w_ref, wg_ref,
         ug_ref, vg_ref, qkv_ref, zs_ref, sga_ref, sgb_ref, gates_ref) = refs
        x = x_ref[0]
    h = _norm(x) * (1.0 + mods_ref[0, 0, 1:2, :]) + mods_ref[0, 0, 0:1, :]
    hb = h.astype(BF16)

    def proj(lo, hi):
        return _dot(hb, w_ref[:, lo:hi])

    sw, gw = seg
    c0 = 0
    ug_ref[0] = _gelu(proj(c0, c0 + sw)).astype(BF16)
    c0 += sw
    vg_ref[0] = _gelu(proj(c0, c0 + sw)).astype(BF16)
    c0 += sw
    for j in range(3):
        qkv_ref[0, :, j * gw:(j + 1) * gw] = proj(c0, c0 + gw).astype(BF16)
        c0 += gw
    zs_ref[0] = _silu(proj(c0, c0 + gw)).astype(BF16)
    c0 += gw
    d = sga_ref.shape[-1]
    half = d // 2
    for j in range(2):
        sga_ref[0, :, j * half:(j + 1) * half] = _sigmoid(proj(c0, c0 + half)).astype(BF16)
        c0 += half
    for j in range(2):
        sgb_ref[0, :, j * half:(j + 1) * half] = _sigmoid(proj(c0, c0 + half)).astype(BF16)
        c0 += half
    gates_ref[0] = _dot(hb, wg_ref[...])


def _inproj(x, w_main, w_gate, mods, n_ctx_blk, prev=None, alpha=1.0):
    bsz, s, d = x.shape
    nb = s // TOK_BLK
    sw = d // 2
    gw = GDN_WIDTH
    n_main = w_main.shape[1]

    def tok(width):
        return pl.BlockSpec((1, TOK_BLK, width), lambda b, i: (b, i, 0))

    def mod_spec():
        return pl.BlockSpec((1, 1, 6, d), lambda b, i: (b, jnp.where(i < n_ctx_blk, 0, 1), 0, 0))

    def full(shape):
        return pl.BlockSpec(shape, lambda b, i: (0,) * len(shape))

    in_specs = [tok(d)]
    args = [x]
    if prev is not None:
        f_prev, mods_prev, ln_g, ln_b = prev
        in_specs += [tok(d), mod_spec(), full((1, d)), full((1, d))]
        args += [f_prev, mods_prev, ln_g.reshape(1, d), ln_b.reshape(1, d)]
    in_specs += [mod_spec(), full((d, n_main)), full((d, LANES))]
    args += [mods, w_main, w_gate]

    out_specs = []
    out_shape = []
    if prev is not None:
        out_specs.append(tok(d))
        out_shape.append(jax.ShapeDtypeStruct((bsz, s, d), F32))
    for width, dt in ((sw, BF16), (sw, BF16), (3 * gw, BF16), (gw, BF16), (d, BF16), (d, BF16),
                      (LANES, F32)):
        out_specs.append(tok(width))
        out_shape.append(jax.ShapeDtypeStruct((bsz, s, width), dt))

    return pl.pallas_call(
        functools.partial(_inproj_kernel, has_prev=prev is not None, alpha=alpha, seg=(sw, gw)),
        grid=(bsz, nb),
        in_specs=in_specs,
        out_specs=out_specs,
        out_shape=out_shape,
        compiler_params=_cparams(("parallel", "arbitrary")),
        name="inproj",
    )(*args)


def _gdn_block(i, n_blk, n_ctx_blk, reverse):
    if not reverse:
        return i
    return jnp.where(i < n_ctx_blk, n_ctx_blk - 1 - i, n_blk - 1 - (i - n_ctx_blk))


def _gdn_kernel(qkv_ref, prev_ref, next_ref, g_ref, gt_ref, cw_ref, gp_ref, gpt_ref,
                o_ref, win_ref, s_ref, *, reverse, n_blk, n_ctx_blk):
    i = pl.program_id(1)
    blk = _gdn_block(i, n_blk, n_ctx_blk, reverse)
    tb = TOK_BLK
    hd = GDN_HEAD_DIM
    nh = GDN_HEADS
    d_idx = 1 if reverse else 0

    @pl.when(i == 0)
    def _():
        s_ref[...] = jnp.zeros_like(s_ref)

    first = jnp.logical_or(blk == 0, blk == n_ctx_blk)
    last = jnp.logical_or(blk == n_ctx_blk - 1, blk == n_blk - 1)
    prev = prev_ref[0].astype(F32)[HALO_ROWS - 8:, :]
    nxt = next_ref[0].astype(F32)[:8, :]
    win_ref[0:8, :] = jnp.where(first, 0.0, prev)
    win_ref[8:8 + tb, :] = qkv_ref[0].astype(F32)
    win_ref[8 + tb:16 + tb, :] = jnp.where(last, 0.0, nxt)

    def conv_slab(col):
        acc = None
        for j in range(CONV_K):
            off = 8 - CONV_K // 2 + j
            term = cw_ref[j:j + 1, col:col + hd] * win_ref[off:off + tb, col:col + hd]
            acc = term if acc is None else acc + term
        return _silu(acc)

    def l2n(x):
        return x * lax.rsqrt(jnp.sum(x * x, axis=-1, keepdims=True) + LN_EPS)

    g = g_ref[0]
    lg_c = -jnp.exp(gp_ref[0:1, :]) * _softplus(g + gp_ref[1:2, :])
    beta_c = _sigmoid(g)
    gt = gt_ref[0]
    lg_r = -jnp.exp(gpt_ref[0][:, 0:1]) * _softplus(gt + gpt_ref[1][:, 0:1])

    ii = lax.broadcasted_iota(jnp.int32, (tb, tb), 0)
    jj = lax.broadcasted_iota(jnp.int32, (tb, tb), 1)
    same_chunk = _blk_id(ii, GDN_CHUNK) == _blk_id(jj, GDN_CHUNK)
    cum = jnp.logical_and(same_chunk, (ii <= jj) if reverse else (ii >= jj))
    cum_c = jnp.where(cum, 1.0, 0.0).astype(BF16)
    cum_r = jnp.where(jnp.logical_and(same_chunk, (jj <= ii) if reverse else (jj >= ii)),
                      1.0, 0.0).astype(BF16)
    ones_c = jnp.where(same_chunk, 1.0, 0.0).astype(BF16)
    gam_c = _dot2_exact_lhs(cum_c, lg_c)
    tot_c = _dot2_exact_lhs(ones_c, lg_c)
    gam_r = _dot2_exact_rhs(lg_r, cum_r)

    sc_rows = 2 * GDN_CHUNK
    pi = lax.broadcasted_iota(jnp.int32, (sc_rows, sc_rows), 0)
    pj = lax.broadcasted_iota(jnp.int32, (sc_rows, sc_rows), 1)
    tri_incl = (pi <= pj) if reverse else (pi >= pj)
    tri_strict = (pi < pj) if reverse else (pi > pj)

    def same(level):
        return _blk_id(pi, level) == _blk_id(pj, level)

    incl = jnp.logical_and(same(GDN_CHUNK), tri_incl)
    strict = jnp.logical_and(same(GDN_CHUNK), tri_strict)
    m16 = jnp.logical_and(strict, same(16))
    m32 = jnp.logical_and(jnp.logical_and(strict, same(32)), jnp.logical_not(same(16)))
    m64 = jnp.logical_and(strict, jnp.logical_not(same(32)))
    eye = jnp.where(pi == pj, 1.0, 0.0)

    n_sc = tb // sc_rows
    sc_order = range(n_sc - 1, -1, -1) if reverse else range(n_sc)
    ch_order = (1, 0) if reverse else (0, 1)

    for h in range(nh):
        q_all = l2n(conv_slab(h * hd)) * (hd ** -0.5)
        k_all = l2n(conv_slab(GDN_WIDTH + h * hd))
        v_all = conv_slab(2 * GDN_WIDTH + h * hd)
        ca = 2 * nh * d_idx + h
        cb = ca + nh
        s = s_ref[h]
        for sc in sc_order:
            r0 = sc * sc_rows
            q = q_all[r0:r0 + sc_rows]
            k = k_all[r0:r0 + sc_rows]
            v = v_all[r0:r0 + sc_rows]
            gcol = gam_c[r0:r0 + sc_rows, ca:ca + 1]
            tcol = tot_c[r0:r0 + sc_rows, ca:ca + 1]
            bcol = beta_c[r0:r0 + sc_rows, cb:cb + 1]
            grow = gam_r[ca:ca + 1, r0:r0 + sc_rows]
            decay = jnp.exp(jnp.where(incl, gcol - grow, -jnp.inf))
            eg = jnp.exp(gcol)
            kb = k.astype(BF16)
            qb = q.astype(BF16)
            kk = _dot_nt(kb, kb)
            neg_l = jnp.where(strict, -(bcol * kk * decay), 0.0)
            x1 = jnp.where(m16, neg_l, 0.0).astype(BF16)
            p = eye + x1.astype(F32)
            x2 = _dot(x1, x1).astype(BF16)
            p = p + _dot(p.astype(BF16), x2)
            x4 = _dot(x2, x2).astype(BF16)
            p = p + _dot(p.astype(BF16), x4)
            x8 = _dot(x4, x4).astype(BF16)
            p = p + _dot(p.astype(BF16), x8)
            for mask in (m32, m64):
                pb = p.astype(BF16)
                e = jnp.where(mask, neg_l, 0.0).astype(BF16)
                p = p + _dot(_dot(pb, e).astype(BF16), pb)
            rhs = jnp.concatenate([bcol * v, (bcol * eg) * k], axis=1).astype(BF16)
            sol = _dot(p.astype(BF16), rhs)
            u = sol[:, :hd]
            wk = sol[:, hd:]
            qk = (_dot_nt(qb, kb) * decay).astype(BF16)
            qg = q * eg
            kd = (k * jnp.exp(tcol - gcol)).astype(BF16)
            wkqg = jnp.concatenate([wk, qg], axis=0).astype(BF16)
            w_parts = [None, None]
            og_parts = [None, None]
            for ch in ch_order:
                c0 = ch * GDN_CHUNK
                sb = s.astype(BF16)
                lhs = jnp.concatenate([wkqg[c0:c0 + GDN_CHUNK],
                                       wkqg[sc_rows + c0:sc_rows + c0 + GDN_CHUNK]], axis=0)
                prod = _dot(lhs, sb)
                w_ch = u[c0:c0 + GDN_CHUNK] - prod[:GDN_CHUNK]
                og_parts[ch] = prod[GDN_CHUNK:]
                w_parts[ch] = w_ch
                gc = jnp.exp(tcol[c0:c0 + 1, :])
                s = gc * s + _dot_tn(kd[c0:c0 + GDN_CHUNK], w_ch.astype(BF16))
            w_full = jnp.concatenate(w_parts, axis=0).astype(BF16)
            o = jnp.concatenate(og_parts, axis=0) + _dot(qk, w_full)
            o_ref[0, r0:r0 + sc_rows, h * hd:(h + 1) * hd] = o.astype(o_ref.dtype)
        s_ref[h] = s


def _gdn(qkv, gates, gates_t, conv_w, gp, gpt, n_ctx_blk, reverse):
    bsz, s, w3 = qkv.shape
    nb = s // TOK_BLK
    halo_per_blk = TOK_BLK // HALO_ROWS
    n_halo = s // HALO_ROWS

    def blk(i):
        return _gdn_block(i, nb, n_ctx_blk, reverse)

    in_specs = [
        pl.BlockSpec((1, TOK_BLK, w3), lambda b, i: (b, blk(i), 0)),
        pl.BlockSpec((1, HALO_ROWS, w3),
                     lambda b, i: (b, jnp.maximum(blk(i) * halo_per_blk - 1, 0), 0)),
        pl.BlockSpec((1, HALO_ROWS, w3),
                     lambda b, i: (b, jnp.minimum((blk(i) + 1) * halo_per_blk, n_halo - 1), 0)),
        pl.BlockSpec((1, TOK_BLK, LANES), lambda b, i: (b, blk(i), 0)),
        pl.BlockSpec((1, 16, TOK_BLK), lambda b, i: (b, 0, blk(i))),
        pl.BlockSpec(conv_w.shape, lambda b, i: (0, 0)),
        pl.BlockSpec(gp.shape, lambda b, i: (0, 0)),
        pl.BlockSpec(gpt.shape, lambda b, i: (0, 0, 0)),
    ]
    return pl.pallas_call(
        functools.partial(_gdn_kernel, reverse=reverse, n_blk=nb, n_ctx_blk=n_ctx_blk),
        grid=(bsz, nb),
        in_specs=in_specs,
        out_specs=pl.BlockSpec((1, TOK_BLK, GDN_WIDTH), lambda b, i: (b, blk(i), 0)),
        out_shape=jax.ShapeDtypeStruct((bsz, s, GDN_WIDTH), BF16),
        scratch_shapes=[pltpu.VMEM((TOK_BLK + 16, w3), F32),
                        pltpu.VMEM((GDN_HEADS, GDN_HEAD_DIM, GDN_HEAD_DIM), F32)],
        compiler_params=_cparams(("parallel", "arbitrary")),
        name="gdn_bwd" if reverse else "gdn_fwd",
    )(qkv, qkv, qkv, gates, gates_t, conv_w, gp, gpt)


def _merge_kernel(*refs, alpha, n_experts):
    (x_ref, ug_ref, vg_ref, of_ref, ob_ref, zs_ref, sga_ref, sgb_ref, mods_ref,
     slg_ref, slb_ref, sw_ref, sbias_ref, nw_ref, wpa_ref, wpb_ref, wo_ref,
     l1g_ref, l1b_ref) = refs[:19]
    if n_experts:
        router_ref, x1_ref, h2_ref, gate_ref = refs[19:]
    else:
        x1_ref, h2_ref = refs[19:]
    tb = TOK_BLK
    vn = (_norm(vg_ref[0].astype(F32)) * slg_ref[...] + slb_ref[...]).astype(BF16)
    gd = vn.shape[1] // SGU_GROUPS
    rows = []
    for c in range(tb // SGU_CHUNK):
        cols = []
        for g in range(SGU_GROUPS):
            cols.append(_dot(sw_ref[g], vn[c * SGU_CHUNK:(c + 1) * SGU_CHUNK, g * gd:(g + 1) * gd]))
        rows.append(jnp.concatenate(cols, axis=1) + sbias_ref[...])
    mixed = jnp.concatenate(rows, axis=0)
    oa = (ug_ref[0].astype(F32) * mixed).astype(BF16)
    osum = of_ref[0].astype(F32) + ob_ref[0].astype(F32)
    heads = []
    for h in range(GDN_HEADS):
        oh = osum[:, h * GDN_HEAD_DIM:(h + 1) * GDN_HEAD_DIM]
        heads.append(oh * lax.rsqrt(jnp.mean(oh * oh, axis=-1, keepdims=True) + LN_EPS) * nw_ref[...])
    ogd = (jnp.concatenate(heads, axis=1) * zs_ref[0].astype(F32)).astype(BF16)
    y = (sga_ref[0].astype(F32) * _dot(oa, wpa_ref[...])
         + sgb_ref[0].astype(F32) * _dot(ogd, wpb_ref[...]))
    t = _dot(y.astype(BF16), wo_ref[...])
    x1 = _norm(alpha * x_ref[0] + mods_ref[0, 0, 2:3, :] * t) * l1g_ref[...] + l1b_ref[...]
    x1_ref[0] = x1
    h2 = _norm(x1) * (1.0 + mods_ref[0, 0, 4:5, :]) + mods_ref[0, 0, 3:4, :]
    h2_ref[0] = h2.astype(BF16)
    if n_experts:
        logits = _dot3(h2, router_ref[...])
        lane = lax.broadcasted_iota(jnp.int32, logits.shape, 1).astype(F32)
        lg = jnp.where(lane < n_experts, logits, -jnp.inf)
        m1 = jnp.max(lg, axis=-1, keepdims=True)
        i1 = jnp.min(jnp.where(lg == m1, lane, float(LANES)), axis=-1, keepdims=True)
        lg2 = jnp.where(lane == i1, -jnp.inf, lg)
        m2 = jnp.max(lg2, axis=-1, keepdims=True)
        i2 = jnp.min(jnp.where(lg2 == m2, lane, float(LANES)), axis=-1, keepdims=True)
        e2 = jnp.exp(m2 - m1)
        den = 1.0 + e2
        gate_ref[0] = jnp.where(lane == i1, 1.0 / den, 0.0) + jnp.where(lane == i2, e2 / den, 0.0)


def _merge(x, ug, vg, o_f, o_b, zs, sga, sgb, mods, sgu_ln_g, sgu_ln_b, sgu_w, sgu_bias, norm_w,
           w_pa, w_pb, w_o, ln_g, ln_b, n_ctx_blk, alpha, router=None):
    bsz, s, d = x.shape
    nb = s // TOK_BLK
    sw = d // 2
    n_experts = 0 if router is None else router[1]

    def tok(width):
        return pl.BlockSpec((1, TOK_BLK, width), lambda b, i: (b, i, 0))

    def full(a):
        return pl.BlockSpec(a.shape, lambda b, i: (0,) * a.ndim)

    consts = [sgu_ln_g.reshape(1, sw), sgu_ln_b.reshape(1, sw), sgu_w, sgu_bias,
              norm_w.reshape(1, GDN_HEAD_DIM), w_pa, w_pb, w_o, ln_g.reshape(1, d), ln_b.reshape(1, d)]
    if n_experts:
        consts.append(router[0])
    in_specs = [tok(d), tok(sw), tok(sw), tok(GDN_WIDTH), tok(GDN_WIDTH), tok(GDN_WIDTH), tok(d), tok(d),
                pl.BlockSpec((1, 1, 6, d), lambda b, i: (b, jnp.where(i < n_ctx_blk, 0, 1), 0, 0))]
    in_specs += [full(a) for a in consts]
    out_specs = [tok(d), tok(d)]
    out_shape = [jax.ShapeDtypeStruct((bsz, s, d), F32), jax.ShapeDtypeStruct((bsz, s, d), BF16)]
    if n_experts:
        out_specs.append(tok(LANES))
        out_shape.append(jax.ShapeDtypeStruct((bsz, s, LANES), F32))
    return pl.pallas_call(
        functools.partial(_merge_kernel, alpha=alpha, n_experts=n_experts),
        grid=(bsz, nb),
        in_specs=in_specs,
        out_specs=out_specs,
        out_shape=out_shape,
        compiler_params=_cparams(("parallel", "arbitrary")),
        name="merge",
    )(x, ug, vg, o_f, o_b, zs, sga, sgb, mods, *consts)


def _ffn_kernel(h_ref, w1_ref, w3_ref, w2_ref, o_ref, acc_ref):
    f = pl.program_id(1)

    @pl.when(f == 0)
    def _():
        acc_ref[...] = jnp.zeros_like(acc_ref)

    h = h_ref[...]
    mid = (_silu(_dot(h, w1_ref[...])) * _dot(h, w3_ref[...])).astype(BF16)
    acc_ref[...] += _dot(mid, w2_ref[...])

    @pl.when(f == pl.num_programs(1) - 1)
    def _():
        o_ref[...] = acc_ref[...].astype(o_ref.dtype)


def _ffn(h, w1, w3, w2, tm, tf):
    t, d = h.shape
    ff = w1.shape[1]
    return pl.pallas_call(
        _ffn_kernel,
        grid=(t // tm, ff // tf),
        in_specs=[pl.BlockSpec((tm, d), lambda i, f: (i, 0)),
                  pl.BlockSpec((d, tf), lambda i, f: (0, f)),
                  pl.BlockSpec((d, tf), lambda i, f: (0, f)),
                  pl.BlockSpec((tf, d), lambda i, f: (f, 0))],
        out_specs=pl.BlockSpec((tm, d), lambda i, f: (i, 0)),
        out_shape=jax.ShapeDtypeStruct((t, d), BF16),
        scratch_shapes=[pltpu.VMEM((tm, d), F32)],
        compiler_params=_cparams(("parallel", "arbitrary")),
        name="ffn",
    )(h, w1, w3, w2)


def _moe_kernel(h_ref, gate_ref, w1_ref, w3_ref, w2_ref, o_ref, acc_ref):
    e = pl.program_id(1)
    f = pl.program_id(2)

    @pl.when(jnp.logical_and(e == 0, f == 0))
    def _():
        acc_ref[...] = jnp.zeros_like(acc_ref)

    gate = gate_ref[...]
    lane = lax.broadcasted_iota(jnp.int32, gate.shape, 1)
    gcol = jnp.sum(jnp.where(lane == e, gate, 0.0), axis=-1, keepdims=True)
    h = h_ref[...]
    mid = (_silu(_dot(h, w1_ref[0])) * _dot(h, w3_ref[0]))
    acc_ref[...] += gcol * _dot(mid.astype(BF16), w2_ref[0])

    @pl.when(jnp.logical_and(e == pl.num_programs(1) - 1, f == pl.num_programs(2) - 1))
    def _():
        o_ref[...] = acc_ref[...].astype(o_ref.dtype)


def _moe(h, gate, w1, w3, w2, tm, tf):
    t, d = h.shape
    ne, _, ff = w1.shape
    return pl.pallas_call(
        _moe_kernel,
        grid=(t // tm, ne, ff // tf),
        in_specs=[pl.BlockSpec((tm, d), lambda i, e, f: (i, 0)),
                  pl.BlockSpec((tm, LANES), lambda i, e, f: (i, 0)),
                  pl.BlockSpec((1, d, tf), lambda i, e, f: (e, 0, f)),
                  pl.BlockSpec((1, d, tf), lambda i, e, f: (e, 0, f)),
                  pl.BlockSpec((1, tf, d), lambda i, e, f: (e, f, 0))],
        out_specs=pl.BlockSpec((tm, d), lambda i, e, f: (i, 0)),
        out_shape=jax.ShapeDtypeStruct((t, d), BF16),
        scratch_shapes=[pltpu.VMEM((tm, d), F32)],
        compiler_params=_cparams(("parallel", "arbitrary", "arbitrary")),
        name="moe",
    )(h, gate, w1, w3, w2)


def _final_kernel(x_ref, f_ref, mods_ref, g_ref, b_ref, o_ref, *, alpha):
    x = alpha * x_ref[0] + mods_ref[0, 0, 5:6, :] * f_ref[0].astype(F32)
    o_ref[0] = _norm(x) * g_ref[...] + b_ref[...]


def _final(x1, f, mods, ln_g, ln_b, n_ctx_blk, alpha):
    bsz, s, d = x1.shape
    nb = s // TOK_BLK - n_ctx_blk
    return pl.pallas_call(
        functools.partial(_final_kernel, alpha=alpha),
        grid=(bsz, nb),
        in_specs=[pl.BlockSpec((1, TOK_BLK, d), lambda b, i: (b, i + n_ctx_blk, 0)),
                  pl.BlockSpec((1, TOK_BLK, d), lambda b, i: (b, i + n_ctx_blk, 0)),
                  pl.BlockSpec((1, 1, 6, d), lambda b, i: (b, 1, 0, 0)),
                  pl.BlockSpec((1, d), lambda b, i: (0, 0)),
                  pl.BlockSpec((1, d), lambda b, i: (0, 0))],
        out_specs=pl.BlockSpec((1, TOK_BLK, d), lambda b, i: (b, i, 0)),
        out_shape=jax.ShapeDtypeStruct((bsz, nb * TOK_BLK, d), F32),
        compiler_params=_cparams(("parallel", "arbitrary")),
        name="final",
    )(x1, f, mods, ln_g.reshape(1, d), ln_b.reshape(1, d))


def _token_tile(t, target):
    best = TOK_BLK
    for m in range(1, t // TOK_BLK + 1):
        cand = m * TOK_BLK
        if cand <= target and t % cand == 0:
            best = cand
    return best


def _ff_tile(ff, target):
    best = LANES
    for m in range(1, ff // LANES + 1):
        cand = m * LANES
        if cand <= target and ff % cand == 0:
            best = cand
    return best


def kernel(x, c, ctx, c_ctx, w_ada, b_ada, w_in, sgu_ln_g, sgu_ln_b, sgu_w, sgu_b, conv_w, a_log, dt_bias, gdn_norm_w, w_pa, w_pb, w_o, ln1_g, ln1_b, ln2_g, ln2_b, ffn_w1, ffn_w3, ffn_w2, moe_router, moe_w1, moe_w3, moe_w2):
    bsz, seq, d = x.shape
    ctx_len = ctx.shape[1]
    depth = w_ada.shape[0]
    alpha = (2.0 * depth) ** 0.25
    s = ctx_len + seq
    assert ctx_len % TOK_BLK == 0 and seq % TOK_BLK == 0
    assert d // 2 == SGU_GROUPS * LANES and bsz < 8
    n_ctx_blk = ctx_len // TOK_BLK
    sw = d // 2
    gw = GDN_WIDTH
    nh = GDN_HEADS

    xs = jnp.concatenate([ctx, x], axis=1)
    cond = jnp.zeros((8, d), F32).at[:bsz].set(c).at[bsz].set(c_ctx)

    t_all = bsz * s
    tm = _token_tile(t_all, 1536)

    f_prev = None
    prev = None
    x_cur = xs
    for l in range(depth):
        mod = _ada(cond, w_ada[l], b_ada[l]).reshape(8, 6, d)
        mods = jnp.stack([jnp.broadcast_to(mod[bsz], (bsz, 6, d)), mod[:bsz]], axis=1)

        g0 = 2 * sw + 4 * gw
        w_main = jnp.concatenate([w_in[l][:, :g0], w_in[l][:, g0 + 4 * nh:]], axis=1).astype(BF16)
        w_gate = jnp.pad(w_in[l][:, g0:g0 + 4 * nh], ((0, 0), (0, LANES - 4 * nh))).astype(BF16)
        outs = _inproj(x_cur, w_main, w_gate, mods, n_ctx_blk, prev=prev, alpha=alpha)
        if prev is not None:
            x_cur, outs = outs[0], outs[1:]
        ug, vg, qkv, zs, sga, sgb, gates = outs

        gates_t = jnp.swapaxes(gates[:, :, :4 * nh], 1, 2)
        al = jnp.zeros((LANES,), F32)
        db = jnp.zeros((LANES,), F32)
        for dd in range(2):
            al = al.at[2 * nh * dd:2 * nh * dd + nh].set(a_log[l, dd])
            db = db.at[2 * nh * dd:2 * nh * dd + nh].set(dt_bias[l, dd])
        gp = jnp.zeros((8, LANES), F32).at[0].set(al).at[1].set(db)
        gpt = jnp.stack([jnp.broadcast_to(al[:16, None], (16, LANES)),
                         jnp.broadcast_to(db[:16, None], (16, LANES))])
        cw = jnp.pad(conv_w[l], ((0, 8 - CONV_K), (0, 0)))
        o_f = _gdn(qkv, gates, gates_t, cw, gp, gpt, n_ctx_blk, reverse=False)
        o_b = _gdn(qkv, gates, gates_t, cw, gp, gpt, n_ctx_blk, reverse=True)

        sgu_bias = jnp.repeat(sgu_b[l].T, LANES, axis=1)
        is_moe = l % 2 == 1
        router = None
        if is_moe:
            r = moe_router[l // 2]
            router = (jnp.pad(r, ((0, 0), (0, LANES - r.shape[1]))), r.shape[1])
        outs = _merge(x_cur, ug, vg, o_f, o_b, zs, sga, sgb, mods, sgu_ln_g[l], sgu_ln_b[l],
                      sgu_w[l].astype(BF16), sgu_bias, gdn_norm_w[l], w_pa[l].astype(BF16),
                      w_pb[l].astype(BF16), w_o[l].astype(BF16), ln1_g[l], ln1_b[l], n_ctx_blk, alpha,
                      router=router)
        x1, h2 = outs[0], outs[1]

        h2f = h2.reshape(t_all, d)
        if is_moe:
            gate = outs[2].reshape(t_all, LANES)
            tf = _ff_tile(moe_w1.shape[-1], 512)
            f = _moe(h2f, gate, moe_w1[l // 2].astype(BF16), moe_w3[l // 2].astype(BF16),
                     moe_w2[l // 2].astype(BF16), tm, tf)
        else:
            tf = _ff_tile(ffn_w1.shape[-1], 512)
            f = _ffn(h2f, ffn_w1[l // 2].astype(BF16), ffn_w3[l // 2].astype(BF16),
                     ffn_w2[l // 2].astype(BF16), tm, tf)
        f = f.reshape(bsz, s, d)
        prev = (f, mods, ln2_g[l], ln2_b[l])
        f_prev, mods_prev = f, mods
        x_cur = x1

    return _final(x_cur, f_prev, mods_prev, ln2_g[depth - 1], ln2_b[depth - 1], n_ctx_blk, alpha)
```

```python
import functools
import math

import jax
import jax.numpy as jnp
from jax import lax
from jax.experimental import pallas as pl
from jax.experimental.pallas import tpu as pltpu

F32 = jnp.float32
BF16 = jnp.bfloat16

LN_EPS = 1e-6
SGU_CHUNK = 128
SGU_GROUPS = 4
GDN_HEADS = 4
GDN_HEAD_DIM = 128
GDN_WIDTH = GDN_HEADS * GDN_HEAD_DIM
GDN_CHUNK = 64
CONV_K = 5
TOP_K = 2

LANES = 128
TOK_BLK = 256
HALO_ROWS = 16
VMEM_LIMIT = 56 * 1024 * 1024


def _cparams(sem):
    return pltpu.CompilerParams(dimension_semantics=sem, vmem_limit_bytes=VMEM_LIMIT)


def _dot(a, b):
    return jnp.dot(a, b, preferred_element_type=F32)


def _dot_nt(a, b):
    return lax.dot_general(a, b, (((1,), (1,)), ((), ())), preferred_element_type=F32)


def _dot_tn(a, b):
    return lax.dot_general(a, b, (((0,), (0,)), ((), ())), preferred_element_type=F32)


def _split(a):
    hi = a.astype(BF16)
    lo = (a - hi.astype(F32)).astype(BF16)
    return hi, lo


def _dot3(a, b):
    ah, al = _split(a)
    bh, bl = _split(b)
    return _dot(ah, bh) + (_dot(ah, bl) + _dot(al, bh))


def _dot2_exact_lhs(a_bf, b):
    bh, bl = _split(b)
    return _dot(a_bf, bh) + _dot(a_bf, bl)


def _dot2_exact_rhs(a, b_bf):
    ah, al = _split(a)
    return _dot(ah, b_bf) + _dot(al, b_bf)


def _norm(x):
    mu = jnp.mean(x, axis=-1, keepdims=True)
    xc = x - mu
    var = jnp.mean(xc * xc, axis=-1, keepdims=True)
    return xc * lax.rsqrt(var + LN_EPS)


def _sigmoid(x):
    return 1.0 / (1.0 + jnp.exp(-x))


def _silu(x):
    return x * _sigmoid(x)


def _gelu(x):
    return 0.5 * x * (1.0 + lax.erf(x * (2.0 ** -0.5)))


def _blk_id(idx, size):
    return jnp.right_shift(idx, int(math.log2(size)))


def _softplus(x):
    return jnp.maximum(x, 0.0) + jnp.log1p(jnp.exp(-jnp.abs(x)))


def _ada_kernel(c_ref, w_ref, b_ref, o_ref):
    o_ref[...] = _dot3(_silu(c_ref[...]), w_ref[...]) + b_ref[...]


def _ada(cond, w, b):
    rows, d = cond.shape
    n = w.shape[1]
    tn = n // 6
    return pl.pallas_call(
        _ada_kernel,
        grid=(n // tn,),
        in_specs=[pl.BlockSpec((rows, d), lambda j: (0, 0)),
                  pl.BlockSpec((d, tn), lambda j: (0, j)),
                  pl.BlockSpec((1, tn), lambda j: (0, j))],
        out_specs=pl.BlockSpec((rows, tn), lambda j: (0, j)),
        out_shape=jax.ShapeDtypeStruct((rows, n), F32),
        compiler_params=_cparams(("arbitrary",)),
        name="ada",
    )(cond, w, b.reshape(1, n))


def _inproj_kernel(*refs, has_prev, alpha, seg):
    if has_prev:
        (x_ref, f_ref, mprev_ref, lng_ref, lnb_ref, mods_ref, w_ref, wg_ref,
         xo_ref, ug_ref, vg_ref, qkv_ref, zs_ref, sga_ref, sgb_ref, gates_ref) = refs
        x = alpha * x_ref[0] + mprev_ref[0, 0, 5:6, :] * f_ref[0].astype(F32)
        x = _norm(x) * lng_ref[...] + lnb_ref[...]
        xo_ref[0] = x
    else:
        (x_ref, mods_ref, w_ref, wg_ref,
         ug_ref, vg_ref, qkv_ref, zs_ref, sga_ref, sgb_ref, gates_ref) = refs
        x = x_ref[0]
    h = _norm(x) * (1.0 + mods_ref[0, 0, 1:2, :]) + mods_ref[0, 0, 0:1, :]
    hb = h.astype(BF16)

    def proj(lo, hi):
        return _dot(hb, w_ref[:, lo:hi])

    sw, gw = seg
    c0 = 0
    ug_ref[0] = _gelu(proj(c0, c0 + sw)).astype(BF16)
    c0 += sw
    vg_ref[0] = _gelu(proj(c0, c0 + sw)).astype(BF16)
    c0 += sw
    for j in range(3):
        qkv_ref[0, :, j * gw:(j + 1) * gw] = proj(c0, c0 + gw).astype(BF16)
        c0 += gw
    zs_ref[0] = _silu(proj(c0, c0 + gw)).astype(BF16)
    c0 += gw
    d = sga_ref.shape[-1]
    half = d // 2
    for j in range(2):
        sga_ref[0, :, j * half:(j + 1) * half] = _sigmoid(proj(c0, c0 + half)).astype(BF16)
        c0 += half
    for j in range(2):
        sgb_ref[0, :, j * half:(j + 1) * half] = _sigmoid(proj(c0, c0 + half)).astype(BF16)
        c0 += half
    gates_ref[0] = _dot(hb, wg_ref[...])


def _inproj(x, w_main, w_gate, mods, n_ctx_blk, prev=None, alpha=1.0):
    bsz, s, d = x.shape
    nb = s // TOK_BLK
    sw = d // 2
    gw = GDN_WIDTH
    n_main = w_main.shape[1]

    def tok(width):
        return pl.BlockSpec((1, TOK_BLK, width), lambda b, i: (b, i, 0))

    def mod_spec():
        return pl.BlockSpec((1, 1, 6, d), lambda b, i: (b, jnp.where(i < n_ctx_blk, 0, 1), 0, 0))

    def full(shape):
        return pl.BlockSpec(shape, lambda b, i: (0,) * len(shape))

    in_specs = [tok(d)]
    args = [x]
    if prev is not None:
        f_prev, mods_prev, ln_g, ln_b = prev
        in_specs += [tok(d), mod_spec(), full((1, d)), full((1, d))]
        args += [f_prev, mods_prev, ln_g.reshape(1, d), ln_b.reshape(1, d)]
    in_specs += [mod_spec(), full((d, n_main)), full((d, LANES))]
    args += [mods, w_main, w_gate]

    out_specs = []
    out_shape = []
    if prev is not None:
        out_specs.append(tok(d))
        out_shape.append(jax.ShapeDtypeStruct((bsz, s, d), F32))
    for width, dt in ((sw, BF16), (sw, BF16), (3 * gw, BF16), (gw, BF16), (d, BF16), (d, BF16),
                      (LANES, F32)):
        out_specs.append(tok(width))
        out_shape.append(jax.ShapeDtypeStruct((bsz, s, width), dt))

    return pl.pallas_call(
        functools.partial(_inproj_kernel, has_prev=prev is not None, alpha=alpha, seg=(sw, gw)),
        grid=(bsz, nb),
        in_specs=in_specs,
        out_specs=out_specs,
        out_shape=out_shape,
        compiler_params=_cparams(("parallel", "arbitrary")),
        name="inproj",
    )(*args)


def _gdn_block(i, n_blk, n_ctx_blk, reverse):
    if not reverse:
        return i
    return jnp.where(i < n_ctx_blk, n_ctx_blk - 1 - i, n_blk - 1 - (i - n_ctx_blk))


def _gdn_kernel(qkv_ref, prev_ref, next_ref, g_ref, gt_ref, cw_ref, gp_ref, gpt_ref,
                o_ref, win_ref, s_ref, *, reverse, n_blk, n_ctx_blk):
    i = pl.program_id(1)
    blk = _gdn_block(i, n_blk, n_ctx_blk, reverse)
    tb = TOK_BLK
    hd = GDN_HEAD_DIM
    nh = GDN_HEADS
    d_idx = 1 if reverse else 0

    @pl.when(i == 0)
    def _():
        s_ref[...] = jnp.zeros_like(s_ref)

    first = jnp.logical_or(blk == 0, blk == n_ctx_blk)
    last = jnp.logical_or(blk == n_ctx_blk - 1, blk == n_blk - 1)
    prev = prev_ref[0].astype(F32)[HALO_ROWS - 8:, :]
    nxt = next_ref[0].astype(F32)[:8, :]
    win_ref[0:8, :] = jnp.where(first, 0.0, prev)
    win_ref[8:8 + tb, :] = qkv_ref[0].astype(F32)
    win_ref[8 + tb:16 + tb, :] = jnp.where(last, 0.0, nxt)

    def conv_slab(col):
        acc = None
        for j in range(CONV_K):
            off = 8 - CONV_K // 2 + j
            term = cw_ref[j:j + 1, col:col + hd] * win_ref[off:off + tb, col:col + hd]
            acc = term if acc is None else acc + term
        return _silu(acc)

    def l2n(x):
        return x * lax.rsqrt(jnp.sum(x * x, axis=-1, keepdims=True) + LN_EPS)

    g = g_ref[0]
    lg_c = -jnp.exp(gp_ref[0:1, :]) * _softplus(g + gp_ref[1:2, :])
    beta_c = _sigmoid(g)
    gt = gt_ref[0]
    lg_r = -jnp.exp(gpt_ref[0][:, 0:1]) * _softplus(gt + gpt_ref[1][:, 0:1])

    ii = lax.broadcasted_iota(jnp.int32, (tb, tb), 0)
    jj = lax.broadcasted_iota(jnp.int32, (tb, tb), 1)
    same_chunk = _blk_id(ii, GDN_CHUNK) == _blk_id(jj, GDN_CHUNK)
    cum = jnp.logical_and(same_chunk, (ii <= jj) if reverse else (ii >= jj))
    cum_c = jnp.where(cum, 1.0, 0.0).astype(BF16)
    cum_r = jnp.where(jnp.logical_and(same_chunk, (jj <= ii) if reverse else (jj >= ii)),
                      1.0, 0.0).astype(BF16)
    ones_c = jnp.where(same_chunk, 1.0, 0.0).astype(BF16)
    gam_c = _dot2_exact_lhs(cum_c, lg_c)
    tot_c = _dot2_exact_lhs(ones_c, lg_c)
    gam_r = _dot2_exact_rhs(lg_r, cum_r)

    sc_rows = 2 * GDN_CHUNK
    pi = lax.broadcasted_iota(jnp.int32, (sc_rows, sc_rows), 0)
    pj = lax.broadcasted_iota(jnp.int32, (sc_rows, sc_rows), 1)
    tri_incl = (pi <= pj) if reverse else (pi >= pj)
    tri_strict = (pi < pj) if reverse else (pi > pj)

    def same(level):
        return _blk_id(pi, level) == _blk_id(pj, level)

    incl = jnp.logical_and(same(GDN_CHUNK), tri_incl)
    strict = jnp.logical_and(same(GDN_CHUNK), tri_strict)
    m16 = jnp.logical_and(strict, same(16))
    m32 = jnp.logical_and(jnp.logical_and(strict, same(32)), jnp.logical_not(same(16)))
    m64 = jnp.logical_and(strict, jnp.logical_not(same(32)))
    eye = jnp.where(pi == pj, 1.0, 0.0)

    n_sc = tb // sc_rows
    sc_order = range(n_sc - 1, -1, -1) if reverse else range(n_sc)
    ch_order = (1, 0) if reverse else (0, 1)

    heads = range(nh)
    tiles = [(h, sc) for h in heads for sc in range(n_sc)]
    q_all = [l2n(conv_slab(h * hd)) * (hd ** -0.5) for h in heads]
    k_all = [l2n(conv_slab(GDN_WIDTH + h * hd)) for h in heads]
    v_all = [conv_slab(2 * GDN_WIDTH + h * hd) for h in heads]

    def rows(a, sc):
        return a[sc * sc_rows:(sc + 1) * sc_rows]

    def col_a(h):
        return 2 * nh * d_idx + h

    q_t = {t: rows(q_all[t[0]], t[1]) for t in tiles}
    k_t = {t: rows(k_all[t[0]], t[1]) for t in tiles}
    v_t = {t: rows(v_all[t[0]], t[1]) for t in tiles}
    gcol = {(h, sc): rows(gam_c, sc)[:, col_a(h):col_a(h) + 1] for h, sc in tiles}
    tcol = {(h, sc): rows(tot_c, sc)[:, col_a(h):col_a(h) + 1] for h, sc in tiles}
    bcol = {(h, sc): rows(beta_c, sc)[:, col_a(h) + nh:col_a(h) + nh + 1] for h, sc in tiles}
    grow = {(h, sc): gam_r[col_a(h):col_a(h) + 1, sc * sc_rows:(sc + 1) * sc_rows] for h, sc in tiles}
    decay = {t: jnp.exp(jnp.where(incl, gcol[t] - grow[t], -jnp.inf)) for t in tiles}
    eg = {t: jnp.exp(gcol[t]) for t in tiles}
    kb = {t: k_t[t].astype(BF16) for t in tiles}
    qb = {t: q_t[t].astype(BF16) for t in tiles}
    kk = {t: _dot_nt(kb[t], kb[t]) for t in tiles}
    qk = {t: (_dot_nt(qb[t], kb[t]) * decay[t]).astype(BF16) for t in tiles}
    neg_l = {t: jnp.where(strict, -(bcol[t] * kk[t] * decay[t]), 0.0) for t in tiles}
    x1 = {t: jnp.where(m16, neg_l[t], 0.0).astype(BF16) for t in tiles}
    p = {t: eye + x1[t].astype(F32) for t in tiles}
    xp = x1
    for _ in range(3):
        xp = {t: _dot(xp[t], xp[t]).astype(BF16) for t in tiles}
        p = {t: p[t] + _dot(p[t].astype(BF16), xp[t]) for t in tiles}
    for mask in (m32, m64):
        pb = {t: p[t].astype(BF16) for t in tiles}
        pe = {t: _dot(pb[t], jnp.where(mask, neg_l[t], 0.0).astype(BF16)).astype(BF16) for t in tiles}
        p = {t: p[t] + _dot(pe[t], pb[t]) for t in tiles}
    rhs = {t: jnp.concatenate([bcol[t] * v_t[t], (bcol[t] * eg[t]) * k_t[t]], axis=1).astype(BF16)
           for t in tiles}
    sol = {t: _dot(p[t].astype(BF16), rhs[t]) for t in tiles}
    kd = {t: (k_t[t] * jnp.exp(tcol[t] - gcol[t])).astype(BF16) for t in tiles}
    wkqg = {t: jnp.concatenate([sol[t][:, hd:], q_t[t] * eg[t]], axis=0).astype(BF16) for t in tiles}

    s = [s_ref[h] for h in heads]
    for sc in sc_order:
        w_parts = [[None, None] for _ in heads]
        og_parts = [[None, None] for _ in heads]
        for ch in ch_order:
            c0 = ch * GDN_CHUNK
            for h in heads:
                t = (h, sc)
                lhs = jnp.concatenate([wkqg[t][c0:c0 + GDN_CHUNK],
                                       wkqg[t][sc_rows + c0:sc_rows + c0 + GDN_CHUNK]], axis=0)
                prod = _dot(lhs, s[h].astype(BF16))
                w_ch = sol[t][c0:c0 + GDN_CHUNK, :hd] - prod[:GDN_CHUNK]
                og_parts[h][ch] = prod[GDN_CHUNK:]
                w_parts[h][ch] = w_ch
                gc = jnp.exp(tcol[t][c0:c0 + 1, :])
                s[h] = gc * s[h] + _dot_tn(kd[t][c0:c0 + GDN_CHUNK], w_ch.astype(BF16))
        for h in heads:
            w_full = jnp.concatenate(w_parts[h], axis=0).astype(BF16)
            o = jnp.concatenate(og_parts[h], axis=0) + _dot(qk[(h, sc)], w_full)
            o_ref[0, sc * sc_rows:(sc + 1) * sc_rows, h * hd:(h + 1) * hd] = o.astype(o_ref.dtype)
    for h in heads:
        s_ref[h] = s[h]


def _gdn(qkv, gates, gates_t, conv_w, gp, gpt, n_ctx_blk, reverse):
    bsz, s, w3 = qkv.shape
    nb = s // TOK_BLK
    halo_per_blk = TOK_BLK // HALO_ROWS
    n_halo = s // HALO_ROWS

    def blk(i):
        return _gdn_block(i, nb, n_ctx_blk, reverse)

    in_specs = [
        pl.BlockSpec((1, TOK_BLK, w3), lambda b, i: (b, blk(i), 0)),
        pl.BlockSpec((1, HALO_ROWS, w3),
                     lambda b, i: (b, jnp.maximum(blk(i) * halo_per_blk - 1, 0), 0)),
        pl.BlockSpec((1, HALO_ROWS, w3),
                     lambda b, i: (b, jnp.minimum((blk(i) + 1) * halo_per_blk, n_halo - 1), 0)),
        pl.BlockSpec((1, TOK_BLK, LANES), lambda b, i: (b, blk(i), 0)),
        pl.BlockSpec((1, 16, TOK_BLK), lambda b, i: (b, 0, blk(i))),
        pl.BlockSpec(conv_w.shape, lambda b, i: (0, 0)),
        pl.BlockSpec(gp.shape, lambda b, i: (0, 0)),
        pl.BlockSpec(gpt.shape, lambda b, i: (0, 0, 0)),
    ]
    return pl.pallas_call(
        functools.partial(_gdn_kernel, reverse=reverse, n_blk=nb, n_ctx_blk=n_ctx_blk),
        grid=(bsz, nb),
        in_specs=in_specs,
        out_specs=pl.BlockSpec((1, TOK_BLK, GDN_WIDTH), lambda b, i: (b, blk(i), 0)),
        out_shape=jax.ShapeDtypeStruct((bsz, s, GDN_WIDTH), BF16),
        scratch_shapes=[pltpu.VMEM((TOK_BLK + 16, w3), F32),
                        pltpu.VMEM((GDN_HEADS, GDN_HEAD_DIM, GDN_HEAD_DIM), F32)],
        compiler_params=_cparams(("parallel", "arbitrary")),
        name="gdn_bwd" if reverse else "gdn_fwd",
    )(qkv, qkv, qkv, gates, gates_t, conv_w, gp, gpt)


def _merge_kernel(*refs, alpha, n_experts):
    (x_ref, ug_ref, vg_ref, of_ref, ob_ref, zs_ref, sga_ref, sgb_ref, mods_ref,
     slg_ref, slb_ref, sw_ref, sbias_ref, nw_ref, wpa_ref, wpb_ref, wo_ref,
     l1g_ref, l1b_ref) = refs[:19]
    if n_experts:
        router_ref, x1_ref, h2_ref, gate_ref, rcol_ref, rrow_ref, cnt_ref = refs[19:]
    else:
        x1_ref, h2_ref = refs[19:]
    tb = TOK_BLK
    vn = (_norm(vg_ref[0].astype(F32)) * slg_ref[...] + slb_ref[...]).astype(BF16)
    gd = vn.shape[1] // SGU_GROUPS
    rows = []
    for c in range(tb // SGU_CHUNK):
        cols = []
        for g in range(SGU_GROUPS):
            cols.append(_dot(sw_ref[g], vn[c * SGU_CHUNK:(c + 1) * SGU_CHUNK, g * gd:(g + 1) * gd]))
        rows.append(jnp.concatenate(cols, axis=1) + sbias_ref[...])
    mixed = jnp.concatenate(rows, axis=0)
    oa = (ug_ref[0].astype(F32) * mixed).astype(BF16)
    osum = of_ref[0].astype(F32) + ob_ref[0].astype(F32)
    heads = []
    for h in range(GDN_HEADS):
        oh = osum[:, h * GDN_HEAD_DIM:(h + 1) * GDN_HEAD_DIM]
        heads.append(oh * lax.rsqrt(jnp.mean(oh * oh, axis=-1, keepdims=True) + LN_EPS) * nw_ref[...])
    ogd = (jnp.concatenate(heads, axis=1) * zs_ref[0].astype(F32)).astype(BF16)
    y = (sga_ref[0].astype(F32) * _dot(oa, wpa_ref[...])
         + sgb_ref[0].astype(F32) * _dot(ogd, wpb_ref[...]))
    t = _dot(y.astype(BF16), wo_ref[...])
    x1 = _norm(alpha * x_ref[0] + mods_ref[0, 0, 2:3, :] * t) * l1g_ref[...] + l1b_ref[...]
    x1_ref[0] = x1
    h2 = _norm(x1) * (1.0 + mods_ref[0, 0, 4:5, :]) + mods_ref[0, 0, 3:4, :]
    h2_ref[0] = h2.astype(BF16)
    if n_experts:
        logits = _dot3(h2, router_ref[...])
        lane = lax.broadcasted_iota(jnp.int32, logits.shape, 1).astype(F32)
        lg = jnp.where(lane < n_experts, logits, -jnp.inf)
        m1 = jnp.max(lg, axis=-1, keepdims=True)
        i1 = jnp.min(jnp.where(lg == m1, lane, float(LANES)), axis=-1, keepdims=True)
        lg2 = jnp.where(lane == i1, -jnp.inf, lg)
        m2 = jnp.max(lg2, axis=-1, keepdims=True)
        i2 = jnp.min(jnp.where(lg2 == m2, lane, float(LANES)), axis=-1, keepdims=True)
        e2 = jnp.exp(m2 - m1)
        den = 1.0 + e2
        gate_ref[0] = jnp.where(lane == i1, 1.0 / den, 0.0) + jnp.where(lane == i2, e2 / den, 0.0)
        sel = jnp.where(jnp.logical_or(lane == i1, lane == i2), 1.0, 0.0)
        selb = sel.astype(BF16)
        ti = lax.broadcasted_iota(jnp.int32, (tb, tb), 0)
        tj = lax.broadcasted_iota(jnp.int32, (tb, tb), 1)
        before = jnp.where(tj < ti, 1.0, 0.0).astype(BF16)
        rcol_ref[0] = jnp.where(sel > 0.0, _dot(before, selb), -1.0)
        er = lax.broadcasted_iota(jnp.int32, (8, LANES), 0)
        ec = lax.broadcasted_iota(jnp.int32, (8, LANES), 1)
        sel_t = _dot_nt(jnp.where(er == ec, 1.0, 0.0).astype(BF16), selb)
        after = jnp.where(ti < tj, 1.0, 0.0).astype(BF16)
        rrow_ref[0, 0] = jnp.where(sel_t > 0.0, _dot(sel_t.astype(BF16), after), -1.0)
        cnt_ref[0, 0] = _dot(jnp.ones((8, tb), BF16), selb)


def _merge(x, ug, vg, o_f, o_b, zs, sga, sgb, mods, sgu_ln_g, sgu_ln_b, sgu_w, sgu_bias, norm_w,
           w_pa, w_pb, w_o, ln_g, ln_b, n_ctx_blk, alpha, router=None):
    bsz, s, d = x.shape
    nb = s // TOK_BLK
    sw = d // 2
    n_experts = 0 if router is None else router[1]

    def tok(width):
        return pl.BlockSpec((1, TOK_BLK, width), lambda b, i: (b, i, 0))

    def full(a):
        return pl.BlockSpec(a.shape, lambda b, i: (0,) * a.ndim)

    consts = [sgu_ln_g.reshape(1, sw), sgu_ln_b.reshape(1, sw), sgu_w, sgu_bias,
              norm_w.reshape(1, GDN_HEAD_DIM), w_pa, w_pb, w_o, ln_g.reshape(1, d), ln_b.reshape(1, d)]
    if n_experts:
        consts.append(router[0])
    in_specs = [tok(d), tok(sw), tok(sw), tok(GDN_WIDTH), tok(GDN_WIDTH), tok(GDN_WIDTH), tok(d), tok(d),
                pl.BlockSpec((1, 1, 6, d), lambda b, i: (b, jnp.where(i < n_ctx_blk, 0, 1), 0, 0))]
    in_specs += [full(a) for a in consts]
    out_specs = [tok(d), tok(d)]
    out_shape = [jax.ShapeDtypeStruct((bsz, s, d), F32), jax.ShapeDtypeStruct((bsz, s, d), BF16)]
    if n_experts:
        assert n_experts <= 8
        out_specs += [tok(LANES), tok(LANES),
                      pl.BlockSpec((1, 1, 8, TOK_BLK), lambda b, i: (b, i, 0, 0)),
                      pl.BlockSpec((1, 1, 8, LANES), lambda b, i: (b, i, 0, 0))]
        out_shape += [jax.ShapeDtypeStruct((bsz, s, LANES), F32), jax.ShapeDtypeStruct((bsz, s, LANES), F32),
                      jax.ShapeDtypeStruct((bsz, nb, 8, TOK_BLK), F32),
                      jax.ShapeDtypeStruct((bsz, nb, 8, LANES), F32)]
    return pl.pallas_call(
        functools.partial(_merge_kernel, alpha=alpha, n_experts=n_experts),
        grid=(bsz, nb),
        in_specs=in_specs,
        out_specs=out_specs,
        out_shape=out_shape,
        compiler_params=_cparams(("parallel", "arbitrary")),
        name="merge",
    )(x, ug, vg, o_f, o_b, zs, sga, sgb, mods, *consts)


def _ffn_kernel(h_ref, w1_ref, w3_ref, w2_ref, o_ref, acc_ref):
    f = pl.program_id(1)

    @pl.when(f == 0)
    def _():
        acc_ref[...] = jnp.zeros_like(acc_ref)

    h = h_ref[...]
    mid = (_silu(_dot(h, w1_ref[...])) * _dot(h, w3_ref[...])).astype(BF16)
    acc_ref[...] += _dot(mid, w2_ref[...])

    @pl.when(f == pl.num_programs(1) - 1)
    def _():
        o_ref[...] = acc_ref[...].astype(o_ref.dtype)


def _ffn(h, w1, w3, w2, tm, tf):
    t, d = h.shape
    ff = w1.shape[1]
    return pl.pallas_call(
        _ffn_kernel,
        grid=(t // tm, ff // tf),
        in_specs=[pl.BlockSpec((tm, d), lambda i, f: (i, 0)),
                  pl.BlockSpec((d, tf), lambda i, f: (0, f)),
                  pl.BlockSpec((d, tf), lambda i, f: (0, f)),
                  pl.BlockSpec((tf, d), lambda i, f: (f, 0))],
        out_specs=pl.BlockSpec((tm, d), lambda i, f: (i, 0)),
        out_shape=jax.ShapeDtypeStruct((t, d), BF16),
        scratch_shapes=[pltpu.VMEM((tm, d), F32)],
        compiler_params=_cparams(("parallel", "arbitrary")),
        name="ffn",
    )(h, w1, w3, w2)


MOE_GATHER_ROWS = 64
MOE_FFN_ROWS = 256
MOE_ROW_ALIGN = 16


def _moe_rows_capacity(tm):
    rows = tm + (tm // TOK_BLK) * MOE_ROW_ALIGN + MOE_GATHER_ROWS
    return -(-rows // MOE_FFN_ROWS) * MOE_FFN_ROWS


def _moe_kernel(cnt_ref, off_ref, tot_ref, h_ref, rcol_ref, gate_ref, rrow_ref, w1_ref, w3_ref, w2_ref,
                o_ref, xs_ref, ys_ref, *, n_experts):
    i = pl.program_id(0)
    e = pl.program_id(1)
    f = pl.program_id(2)
    tm = h_ref.shape[0]
    n_sub = tm // TOK_BLK
    gr = MOE_GATHER_ROWS
    fr = MOE_FFN_ROWS
    n_ffn_blk = _blk_id(tot_ref[i * n_experts + e] + (fr - 1), fr)

    def piece(j):
        base = (i * n_sub + j) * n_experts + e
        off = pl.multiple_of(off_ref[base], MOE_ROW_ALIGN)
        return off, _blk_id(cnt_ref[base] + (gr - 1), gr)

    @pl.when(jnp.logical_and(e == 0, f == 0))
    def _():
        o_ref[...] = jnp.zeros_like(o_ref)
        xs_ref[...] = jnp.zeros_like(xs_ref)

    @pl.when(f == 0)
    def _():
        row_id = lax.broadcasted_iota(jnp.int32, (gr, TOK_BLK), 0).astype(F32)
        for j in range(n_sub):
            off, n_blk = piece(j)
            rank = rrow_ref[j, pl.ds(e, 1), :]

            def gather(b, carry, j=j, off=off, rank=rank):
                onehot = jnp.where(rank == row_id + (b * gr).astype(F32), 1.0, 0.0).astype(BF16)
                rows = _dot(onehot, h_ref[j * TOK_BLK:(j + 1) * TOK_BLK, :])
                xs_ref[pl.ds(pl.multiple_of(off + b * gr, MOE_ROW_ALIGN), gr), :] = rows.astype(BF16)
                return carry

            lax.fori_loop(0, n_blk, gather, 0)

        def clear(rb, carry):
            ys_ref[pl.ds(pl.multiple_of(rb * fr, fr), fr), :] = jnp.zeros((fr, ys_ref.shape[1]), F32)
            return carry

        lax.fori_loop(0, n_ffn_blk, clear, 0)

    def ffn(rb, carry):
        r0 = pl.multiple_of(rb * fr, fr)
        x = xs_ref[pl.ds(r0, fr), :]
        mid = (_silu(_dot(x, w1_ref[0])) * _dot(x, w3_ref[0])).astype(BF16)
        ys_ref[pl.ds(r0, fr), :] += _dot(mid, w2_ref[0])
        return carry

    lax.fori_loop(0, n_ffn_blk, ffn, 0)

    @pl.when(f == pl.num_programs(2) - 1)
    def _():
        col_id = lax.broadcasted_iota(jnp.int32, (TOK_BLK, gr), 1).astype(F32)
        lane = lax.broadcasted_iota(jnp.int32, (TOK_BLK, LANES), 1)
        for j in range(n_sub):
            off, n_blk = piece(j)
            tok = slice(j * TOK_BLK, (j + 1) * TOK_BLK)
            rank = jnp.sum(jnp.where(lane == e, rcol_ref[tok, :], 0.0), axis=-1, keepdims=True)
            wgt = jnp.sum(jnp.where(lane == e, gate_ref[tok, :], 0.0), axis=-1, keepdims=True)

            def scatter(b, acc, off=off, rank=rank):
                onehot = jnp.where(rank == col_id + (b * gr).astype(F32), 1.0, 0.0).astype(BF16)
                y = ys_ref[pl.ds(pl.multiple_of(off + b * gr, MOE_ROW_ALIGN), gr), :]
                return acc + _dot(onehot, y.astype(BF16))

            acc = lax.fori_loop(0, n_blk, scatter, jnp.zeros((TOK_BLK, o_ref.shape[1]), F32))
            o_ref[tok, :] += wgt * acc


def _moe(h, rcol, gate, rrow, cnt, off, tot, w1, w3, w2, tm, tf):
    t, d = h.shape
    ne, _, ff = w1.shape
    n_sub = tm // TOK_BLK
    cap = _moe_rows_capacity(tm)
    grid_spec = pltpu.PrefetchScalarGridSpec(
        num_scalar_prefetch=3,
        grid=(t // tm, ne, ff // tf),
        in_specs=[pl.BlockSpec((tm, d), lambda i, e, f, *_: (i, 0)),
                  pl.BlockSpec((tm, LANES), lambda i, e, f, *_: (i, 0)),
                  pl.BlockSpec((tm, LANES), lambda i, e, f, *_: (i, 0)),
                  pl.BlockSpec((n_sub, 8, TOK_BLK), lambda i, e, f, *_: (i, 0, 0)),
                  pl.BlockSpec((1, d, tf), lambda i, e, f, *_: (e, 0, f)),
                  pl.BlockSpec((1, d, tf), lambda i, e, f, *_: (e, 0, f)),
                  pl.BlockSpec((1, tf, d), lambda i, e, f, *_: (e, f, 0))],
        out_specs=pl.BlockSpec((tm, d), lambda i, e, f, *_: (i, 0)),
        scratch_shapes=[pltpu.VMEM((cap, d), BF16), pltpu.VMEM((cap, d), F32)],
    )
    return pl.pallas_call(
        functools.partial(_moe_kernel, n_experts=ne),
        grid_spec=grid_spec,
        out_shape=jax.ShapeDtypeStruct((t, d), F32),
        compiler_params=_cparams(("parallel", "arbitrary", "arbitrary")),
        name="moe",
    )(cnt, off, tot, h, rcol, gate, rrow, w1, w3, w2)


def _final_kernel(x_ref, f_ref, mods_ref, g_ref, b_ref, o_ref, *, alpha):
    x = alpha * x_ref[0] + mods_ref[0, 0, 5:6, :] * f_ref[0].astype(F32)
    o_ref[0] = _norm(x) * g_ref[...] + b_ref[...]


def _final(x1, f, mods, ln_g, ln_b, n_ctx_blk, alpha):
    bsz, s, d = x1.shape
    nb = s // TOK_BLK - n_ctx_blk
    return pl.pallas_call(
        functools.partial(_final_kernel, alpha=alpha),
        grid=(bsz, nb),
        in_specs=[pl.BlockSpec((1, TOK_BLK, d), lambda b, i: (b, i + n_ctx_blk, 0)),
                  pl.BlockSpec((1, TOK_BLK, d), lambda b, i: (b, i + n_ctx_blk, 0)),
                  pl.BlockSpec((1, 1, 6, d), lambda b, i: (b, 1, 0, 0)),
                  pl.BlockSpec((1, d), lambda b, i: (0, 0)),
                  pl.BlockSpec((1, d), lambda b, i: (0, 0))],
        out_specs=pl.BlockSpec((1, TOK_BLK, d), lambda b, i: (b, i, 0)),
        out_shape=jax.ShapeDtypeStruct((bsz, nb * TOK_BLK, d), F32),
        compiler_params=_cparams(("parallel", "arbitrary")),
        name="final",
    )(x1, f, mods, ln_g.reshape(1, d), ln_b.reshape(1, d))


def _token_tile(t, target):
    best = TOK_BLK
    for m in range(1, t // TOK_BLK + 1):
        cand = m * TOK_BLK
        if cand <= target and t % cand == 0:
            best = cand
    return best


def _ff_tile(ff, target):
    best = LANES
    for m in range(1, ff // LANES + 1):
        cand = m * LANES
        if cand <= target and ff % cand == 0:
            best = cand
    return best


def kernel(x, c, ctx, c_ctx, w_ada, b_ada, w_in, sgu_ln_g, sgu_ln_b, sgu_w, sgu_b, conv_w, a_log, dt_bias, gdn_norm_w, w_pa, w_pb, w_o, ln1_g, ln1_b, ln2_g, ln2_b, ffn_w1, ffn_w3, ffn_w2, moe_router, moe_w1, moe_w3, moe_w2):
    bsz, seq, d = x.shape
    ctx_len = ctx.shape[1]
    depth = w_ada.shape[0]
    alpha = (2.0 * depth) ** 0.25
    s = ctx_len + seq
    assert ctx_len % TOK_BLK == 0 and seq % TOK_BLK == 0
    assert d // 2 == SGU_GROUPS * LANES and bsz < 8
    n_ctx_blk = ctx_len // TOK_BLK
    sw = d // 2
    gw = GDN_WIDTH
    nh = GDN_HEADS

    xs = jnp.concatenate([ctx, x], axis=1)
    cond = jnp.zeros((8, d), F32).at[:bsz].set(c).at[bsz].set(c_ctx)

    t_all = bsz * s
    tm = _token_tile(t_all, 1536)

    f_prev = None
    prev = None
    x_cur = xs
    for l in range(depth):
        mod = _ada(cond, w_ada[l], b_ada[l]).reshape(8, 6, d)
        mods = jnp.stack([jnp.broadcast_to(mod[bsz], (bsz, 6, d)), mod[:bsz]], axis=1)

        g0 = 2 * sw + 4 * gw
        w_main = jnp.concatenate([w_in[l][:, :g0], w_in[l][:, g0 + 4 * nh:]], axis=1).astype(BF16)
        w_gate = jnp.pad(w_in[l][:, g0:g0 + 4 * nh], ((0, 0), (0, LANES - 4 * nh))).astype(BF16)
        outs = _inproj(x_cur, w_main, w_gate, mods, n_ctx_blk, prev=prev, alpha=alpha)
        if prev is not None:
            x_cur, outs = outs[0], outs[1:]
        ug, vg, qkv, zs, sga, sgb, gates = outs

        gates_t = jnp.swapaxes(gates[:, :, :4 * nh], 1, 2)
        al = jnp.zeros((LANES,), F32)
        db = jnp.zeros((LANES,), F32)
        for dd in range(2):
            al = al.at[2 * nh * dd:2 * nh * dd + nh].set(a_log[l, dd])
            db = db.at[2 * nh * dd:2 * nh * dd + nh].set(dt_bias[l, dd])
        gp = jnp.zeros((8, LANES), F32).at[0].set(al).at[1].set(db)
        gpt = jnp.stack([jnp.broadcast_to(al[:16, None], (16, LANES)),
                         jnp.broadcast_to(db[:16, None], (16, LANES))])
        cw = jnp.pad(conv_w[l], ((0, 8 - CONV_K), (0, 0)))
        o_f = _gdn(qkv, gates, gates_t, cw, gp, gpt, n_ctx_blk, reverse=False)
        o_b = _gdn(qkv, gates, gates_t, cw, gp, gpt, n_ctx_blk, reverse=True)

        sgu_bias = jnp.repeat(sgu_b[l].T, LANES, axis=1)
        is_moe = l % 2 == 1
        router = None
        if is_moe:
            r = moe_router[l // 2]
            router = (jnp.pad(r, ((0, 0), (0, LANES - r.shape[1]))), r.shape[1])
        outs = _merge(x_cur, ug, vg, o_f, o_b, zs, sga, sgb, mods, sgu_ln_g[l], sgu_ln_b[l],
                      sgu_w[l].astype(BF16), sgu_bias, gdn_norm_w[l], w_pa[l].astype(BF16),
                      w_pb[l].astype(BF16), w_o[l].astype(BF16), ln1_g[l], ln1_b[l], n_ctx_blk, alpha,
                      router=router)
        x1, h2 = outs[0], outs[1]

        h2f = h2.reshape(t_all, d)
        if is_moe:
            ne = moe_router.shape[-1]
            gate = outs[2].reshape(t_all, LANES)
            rcol = outs[3].reshape(t_all, LANES)
            rrow = outs[4].reshape(t_all // TOK_BLK, 8, TOK_BLK)
            n_sub = tm // TOK_BLK
            cnt = outs[5][:, :, 0, :ne].astype(jnp.int32).reshape(t_all // tm, n_sub, ne)
            aligned = (cnt + (MOE_ROW_ALIGN - 1)) // MOE_ROW_ALIGN * MOE_ROW_ALIGN
            off = jnp.cumsum(aligned, axis=1) - aligned
            tot = jnp.sum(aligned, axis=1)
            tf = _ff_tile(moe_w1.shape[-1], 512)
            f = _moe(h2f, rcol, gate, rrow, cnt.reshape(-1), off.reshape(-1), tot.reshape(-1),
                     moe_w1[l // 2].astype(BF16), moe_w3[l // 2].astype(BF16),
                     moe_w2[l // 2].astype(BF16), tm, tf)
        else:
            tf = _ff_tile(ffn_w1.shape[-1], 512)
            f = _ffn(h2f, ffn_w1[l // 2].astype(BF16), ffn_w3[l // 2].astype(BF16),
                     ffn_w2[l // 2].astype(BF16), tm, tf)
        f = f.reshape(bsz, s, d)
        prev = (f, mods, ln2_g[l], ln2_b[l])
        f_prev, mods_prev = f, mods
        x_cur = x1

    return _final(x_cur, f_prev, mods_prev, ln2_g[depth - 1], ln2_b[depth - 1], n_ctx_blk, alpha)
```

```python
import functools
import math

import jax
import jax.numpy as jnp
from jax import lax
from jax.experimental import pallas as pl
from jax.experimental.pallas import tpu as pltpu

F32 = jnp.float32
BF16 = jnp.bfloat16

LN_EPS = 1e-6
SGU_CHUNK = 128
SGU_GROUPS = 4
GDN_HEADS = 4
GDN_HEAD_DIM = 128
GDN_WIDTH = GDN_HEADS * GDN_HEAD_DIM
GDN_CHUNK = 64
CONV_K = 5
TOP_K = 2

LANES = 128
TOK_BLK = 256
HALO_ROWS = 16
VMEM_LIMIT = 56 * 1024 * 1024


def _cparams(sem):
    return pltpu.CompilerParams(dimension_semantics=sem, vmem_limit_bytes=VMEM_LIMIT)


def _dot(a, b):
    return jnp.dot(a, b, preferred_element_type=F32)


def _dot_nt(a, b):
    return lax.dot_general(a, b, (((1,), (1,)), ((), ())), preferred_element_type=F32)


def _dot_tn(a, b):
    return lax.dot_general(a, b, (((0,), (0,)), ((), ())), preferred_element_type=F32)


def _split(a):
    hi = a.astype(BF16)
    lo = (a - hi.astype(F32)).astype(BF16)
    return hi, lo


def _dot3(a, b):
    ah, al = _split(a)
    bh, bl = _split(b)
    return _dot(ah, bh) + (_dot(ah, bl) + _dot(al, bh))


def _dot2_exact_lhs(a_bf, b):
    bh, bl = _split(b)
    return _dot(a_bf, bh) + _dot(a_bf, bl)


def _dot2_exact_rhs(a, b_bf):
    ah, al = _split(a)
    return _dot(ah, b_bf) + _dot(al, b_bf)


def _norm(x):
    mu = jnp.mean(x, axis=-1, keepdims=True)
    xc = x - mu
    var = jnp.mean(xc * xc, axis=-1, keepdims=True)
    return xc * lax.rsqrt(var + LN_EPS)


def _sigmoid(x):
    return 1.0 / (1.0 + jnp.exp(-x))


def _silu(x):
    return x * _sigmoid(x)


def _gelu(x):
    return 0.5 * x * (1.0 + lax.erf(x * (2.0 ** -0.5)))


def _blk_id(idx, size):
    return jnp.right_shift(idx, int(math.log2(size)))


def _softplus(x):
    return jnp.maximum(x, 0.0) + jnp.log1p(jnp.exp(-jnp.abs(x)))


def _ada_kernel(c_ref, w_ref, b_ref, o_ref):
    o_ref[...] = _dot3(_silu(c_ref[...]), w_ref[...]) + b_ref[...]


def _ada(cond, w, b):
    rows, d = cond.shape
    n = w.shape[1]
    tn = n // 6
    return pl.pallas_call(
        _ada_kernel,
        grid=(n // tn,),
        in_specs=[pl.BlockSpec((rows, d), lambda j: (0, 0)),
                  pl.BlockSpec((d, tn), lambda j: (0, j)),
                  pl.BlockSpec((1, tn), lambda j: (0, j))],
        out_specs=pl.BlockSpec((rows, tn), lambda j: (0, j)),
        out_shape=jax.ShapeDtypeStruct((rows, n), F32),
        compiler_params=_cparams(("arbitrary",)),
        name="ada",
    )(cond, w, b.reshape(1, n))


def _inproj_kernel(*refs, has_prev, alpha, seg):
    if has_prev:
        (x_ref, f_ref, mprev_ref, lng_ref, lnb_ref, mods_ref, w_ref, wg_ref,
         xo_ref, ug_ref, vg_ref, qkv_ref, zs_ref, sga_ref, sgb_ref, gates_ref) = refs
        x = alpha * x_ref[0] + mprev_ref[0, 0, 5:6, :] * f_ref[0].astype(F32)
        x = _norm(x) * lng_ref[...] + lnb_ref[...]
        xo_ref[0] = x
    else:
        (x_ref, mods_ref, w_ref, wg_ref,
         ug_ref, vg_ref, qkv_ref, zs_ref, sga_ref, sgb_ref, gates_ref) = refs
        x = x_ref[0]
    h = _norm(x) * (1.0 + mods_ref[0, 0, 1:2, :]) + mods_ref[0, 0, 0:1, :]
    hb = h.astype(BF16)

    def proj(lo, hi):
        return _dot(hb, w_ref[:, lo:hi])

    sw, gw = seg
    c0 = 0
    ug_ref[0] = _gelu(proj(c0, c0 + sw)).astype(BF16)
    c0 += sw
    vg_ref[0] = _gelu(proj(c0, c0 + sw)).astype(BF16)
    c0 += sw
    for j in range(3):
        qkv_ref[0, :, j * gw:(j + 1) * gw] = proj(c0, c0 + gw).astype(BF16)
        c0 += gw
    zs_ref[0] = _silu(proj(c0, c0 + gw)).astype(BF16)
    c0 += gw
    d = sga_ref.shape[-1]
    half = d // 2
    for j in range(2):
        sga_ref[0, :, j * half:(j + 1) * half] = _sigmoid(proj(c0, c0 + half)).astype(BF16)
        c0 += half
    for j in range(2):
        sgb_ref[0, :, j * half:(j + 1) * half] = _sigmoid(proj(c0, c0 + half)).astype(BF16)
        c0 += half
    gates_ref[0] = _dot(hb, wg_ref[...])


def _inproj(x, w_main, w_gate, mods, n_ctx_blk, prev=None, alpha=1.0):
    bsz, s, d = x.shape
    nb = s // TOK_BLK
    sw = d // 2
    gw = GDN_WIDTH
    n_main = w_main.shape[1]

    def tok(width):
        return pl.BlockSpec((1, TOK_BLK, width), lambda b, i: (b, i, 0))

    def mod_spec():
        return pl.BlockSpec((1, 1, 6, d), lambda b, i: (b, jnp.where(i < n_ctx_blk, 0, 1), 0, 0))

    def full(shape):
        return pl.BlockSpec(shape, lambda b, i: (0,) * len(shape))

    in_specs = [tok(d)]
    args = [x]
    if prev is not None:
        f_prev, mods_prev, ln_g, ln_b = prev
        in_specs += [tok(d), mod_spec(), full((1, d)), full((1, d))]
        args += [f_prev, mods_prev, ln_g.reshape(1, d), ln_b.reshape(1, d)]
    in_specs += [mod_spec(), full((d, n_main)), full((d, LANES))]
    args += [mods, w_main, w_gate]

    out_specs = []
    out_shape = []
    if prev is not None:
        out_specs.append(tok(d))
        out_shape.append(jax.ShapeDtypeStruct((bsz, s, d), F32))
    for width, dt in ((sw, BF16), (sw, BF16), (3 * gw, BF16), (gw, BF16), (d, BF16), (d, BF16),
                      (LANES, F32)):
        out_specs.append(tok(width))
        out_shape.append(jax.ShapeDtypeStruct((bsz, s, width), dt))

    return pl.pallas_call(
        functools.partial(_inproj_kernel, has_prev=prev is not None, alpha=alpha, seg=(sw, gw)),
        grid=(bsz, nb),
        in_specs=in_specs,
        out_specs=out_specs,
        out_shape=out_shape,
        compiler_params=_cparams(("parallel", "arbitrary")),
        name="inproj",
    )(*args)


def _gdn_block(i, n_blk, n_ctx_blk, reverse):
    if not reverse:
        return i
    return jnp.where(i < n_ctx_blk, n_ctx_blk - 1 - i, n_blk - 1 - (i - n_ctx_blk))


def _gdn_kernel(*refs, reverse, n_blk, n_ctx_blk):
    if reverse:
        qkvn_ref, g_ref, gt_ref, gp_ref, gpt_ref, o_ref, s_ref = refs
    else:
        (qkv_ref, prev_ref, next_ref, g_ref, gt_ref, cw_ref, gp_ref, gpt_ref,
         o_ref, qkvn_ref, win_ref, s_ref) = refs
    i = pl.program_id(1)
    blk = _gdn_block(i, n_blk, n_ctx_blk, reverse)
    tb = TOK_BLK
    hd = GDN_HEAD_DIM
    nh = GDN_HEADS
    d_idx = 1 if reverse else 0

    @pl.when(i == 0)
    def _():
        s_ref[...] = jnp.zeros_like(s_ref)

    if not reverse:
        first = jnp.logical_or(blk == 0, blk == n_ctx_blk)
        last = jnp.logical_or(blk == n_ctx_blk - 1, blk == n_blk - 1)
        prev = prev_ref[0].astype(F32)[HALO_ROWS - 8:, :]
        nxt = next_ref[0].astype(F32)[:8, :]
        win_ref[0:8, :] = jnp.where(first, 0.0, prev)
        win_ref[8:8 + tb, :] = qkv_ref[0].astype(F32)
        win_ref[8 + tb:16 + tb, :] = jnp.where(last, 0.0, nxt)

    def conv_slab(col):
        acc = None
        for j in range(CONV_K):
            off = 8 - CONV_K // 2 + j
            term = cw_ref[j:j + 1, col:col + hd] * win_ref[off:off + tb, col:col + hd]
            acc = term if acc is None else acc + term
        return _silu(acc)

    def l2n(x):
        return x * lax.rsqrt(jnp.sum(x * x, axis=-1, keepdims=True) + LN_EPS)

    def qkv_slab(col, kind):
        if reverse:
            return qkvn_ref[0, :, col:col + hd].astype(F32)
        x = conv_slab(col)
        if kind == 0:
            x = l2n(x) * (hd ** -0.5)
        elif kind == 1:
            x = l2n(x)
        qkvn_ref[0, :, col:col + hd] = x.astype(BF16)
        return x

    g = g_ref[0]
    lg_c = -jnp.exp(gp_ref[0:1, :]) * _softplus(g + gp_ref[1:2, :])
    beta_c = _sigmoid(g)
    gt = gt_ref[0]
    lg_r = -jnp.exp(gpt_ref[0][:, 0:1]) * _softplus(gt + gpt_ref[1][:, 0:1])

    ii = lax.broadcasted_iota(jnp.int32, (tb, tb), 0)
    jj = lax.broadcasted_iota(jnp.int32, (tb, tb), 1)
    same_chunk = _blk_id(ii, GDN_CHUNK) == _blk_id(jj, GDN_CHUNK)
    cum = jnp.logical_and(same_chunk, (ii <= jj) if reverse else (ii >= jj))
    cum_c = jnp.where(cum, 1.0, 0.0).astype(BF16)
    cum_r = jnp.where(jnp.logical_and(same_chunk, (jj <= ii) if reverse else (jj >= ii)),
                      1.0, 0.0).astype(BF16)
    ones_c = jnp.where(same_chunk, 1.0, 0.0).astype(BF16)
    gam_c = _dot2_exact_lhs(cum_c, lg_c)
    tot_c = _dot2_exact_lhs(ones_c, lg_c)
    gam_r = _dot2_exact_rhs(lg_r, cum_r)

    sc_rows = 2 * GDN_CHUNK
    pi = lax.broadcasted_iota(jnp.int32, (sc_rows, sc_rows), 0)
    pj = lax.broadcasted_iota(jnp.int32, (sc_rows, sc_rows), 1)
    tri_incl = (pi <= pj) if reverse else (pi >= pj)
    tri_strict = (pi < pj) if reverse else (pi > pj)

    def same(level):
        return _blk_id(pi, level) == _blk_id(pj, level)

    incl = jnp.logical_and(same(GDN_CHUNK), tri_incl)
    strict = jnp.logical_and(same(GDN_CHUNK), tri_strict)
    m16 = jnp.logical_and(strict, same(16))
    m32 = jnp.logical_and(jnp.logical_and(strict, same(32)), jnp.logical_not(same(16)))
    m64 = jnp.logical_and(strict, jnp.logical_not(same(32)))
    eye = jnp.where(pi == pj, 1.0, 0.0)

    n_sc = tb // sc_rows
    sc_order = range(n_sc - 1, -1, -1) if reverse else range(n_sc)
    ch_order = (1, 0) if reverse else (0, 1)

    heads = range(nh)
    tiles = [(h, sc) for h in heads for sc in range(n_sc)]
    q_all = [qkv_slab(h * hd, 0) for h in heads]
    k_all = [qkv_slab(GDN_WIDTH + h * hd, 1) for h in heads]
    v_all = [qkv_slab(2 * GDN_WIDTH + h * hd, 2) for h in heads]

    def rows(a, sc):
        return a[sc * sc_rows:(sc + 1) * sc_rows]

    def col_a(h):
        return 2 * nh * d_idx + h

    q_t = {t: rows(q_all[t[0]], t[1]) for t in tiles}
    k_t = {t: rows(k_all[t[0]], t[1]) for t in tiles}
    v_t = {t: rows(v_all[t[0]], t[1]) for t in tiles}
    gcol = {(h, sc): rows(gam_c, sc)[:, col_a(h):col_a(h) + 1] for h, sc in tiles}
    tcol = {(h, sc): rows(tot_c, sc)[:, col_a(h):col_a(h) + 1] for h, sc in tiles}
    bcol = {(h, sc): rows(beta_c, sc)[:, col_a(h) + nh:col_a(h) + nh + 1] for h, sc in tiles}
    grow = {(h, sc): gam_r[col_a(h):col_a(h) + 1, sc * sc_rows:(sc + 1) * sc_rows] for h, sc in tiles}
    decay = {t: jnp.exp(jnp.where(incl, gcol[t] - grow[t], -jnp.inf)) for t in tiles}
    eg = {t: jnp.exp(gcol[t]) for t in tiles}
    kb = {t: k_t[t].astype(BF16) for t in tiles}
    qb = {t: q_t[t].astype(BF16) for t in tiles}
    kk = {t: _dot_nt(kb[t], kb[t]) for t in tiles}
    qk = {t: (_dot_nt(qb[t], kb[t]) * decay[t]).astype(BF16) for t in tiles}
    neg_l = {t: jnp.where(strict, -(bcol[t] * kk[t] * decay[t]), 0.0) for t in tiles}
    x1 = {t: jnp.where(m16, neg_l[t], 0.0).astype(BF16) for t in tiles}
    p = {t: eye + x1[t].astype(F32) for t in tiles}
    xp = x1
    for _ in range(3):
        xp = {t: _dot(xp[t], xp[t]).astype(BF16) for t in tiles}
        p = {t: p[t] + _dot(p[t].astype(BF16), xp[t]) for t in tiles}
    for mask in (m32, m64):
        pb = {t: p[t].astype(BF16) for t in tiles}
        pe = {t: _dot(pb[t], jnp.where(mask, neg_l[t], 0.0).astype(BF16)).astype(BF16) for t in tiles}
        p = {t: p[t] + _dot(pe[t], pb[t]) for t in tiles}
    rhs = {t: jnp.concatenate([bcol[t] * v_t[t], (bcol[t] * eg[t]) * k_t[t]], axis=1).astype(BF16)
           for t in tiles}
    sol = {t: _dot(p[t].astype(BF16), rhs[t]) for t in tiles}
    kd = {t: (k_t[t] * jnp.exp(tcol[t] - gcol[t])).astype(BF16) for t in tiles}
    wkqg = {t: jnp.concatenate([sol[t][:, hd:], q_t[t] * eg[t]], axis=0).astype(BF16) for t in tiles}

    s = [s_ref[h] for h in heads]
    for sc in sc_order:
        w_parts = [[None, None] for _ in heads]
        og_parts = [[None, None] for _ in heads]
        for ch in ch_order:
            c0 = ch * GDN_CHUNK
            for h in heads:
                t = (h, sc)
                lhs = jnp.concatenate([wkqg[t][c0:c0 + GDN_CHUNK],
                                       wkqg[t][sc_rows + c0:sc_rows + c0 + GDN_CHUNK]], axis=0)
                prod = _dot(lhs, s[h].astype(BF16))
                w_ch = sol[t][c0:c0 + GDN_CHUNK, :hd] - prod[:GDN_CHUNK]
                og_parts[h][ch] = prod[GDN_CHUNK:]
                w_parts[h][ch] = w_ch
                gc = jnp.exp(tcol[t][c0:c0 + 1, :])
                s[h] = gc * s[h] + _dot_tn(kd[t][c0:c0 + GDN_CHUNK], w_ch.astype(BF16))
        for h in heads:
            w_full = jnp.concatenate(w_parts[h], axis=0).astype(BF16)
            o = jnp.concatenate(og_parts[h], axis=0) + _dot(qk[(h, sc)], w_full)
            o_ref[0, sc * sc_rows:(sc + 1) * sc_rows, h * hd:(h + 1) * hd] = o.astype(o_ref.dtype)
    for h in heads:
        s_ref[h] = s[h]


def _gdn(qkv, gates, gates_t, conv_w, gp, gpt, n_ctx_blk, reverse):
    bsz, s, w3 = qkv.shape
    nb = s // TOK_BLK
    halo_per_blk = TOK_BLK // HALO_ROWS
    n_halo = s // HALO_ROWS

    def blk(i):
        return _gdn_block(i, nb, n_ctx_blk, reverse)

    tok_qkv = pl.BlockSpec((1, TOK_BLK, w3), lambda b, i: (b, blk(i), 0))
    tok_o = pl.BlockSpec((1, TOK_BLK, GDN_WIDTH), lambda b, i: (b, blk(i), 0))
    gate_specs = [pl.BlockSpec((1, TOK_BLK, LANES), lambda b, i: (b, blk(i), 0)),
                  pl.BlockSpec((1, 16, TOK_BLK), lambda b, i: (b, 0, blk(i)))]
    par_specs = [pl.BlockSpec(gp.shape, lambda b, i: (0, 0)),
                 pl.BlockSpec(gpt.shape, lambda b, i: (0, 0, 0))]
    state = pltpu.VMEM((GDN_HEADS, GDN_HEAD_DIM, GDN_HEAD_DIM), F32)
    o_shape = jax.ShapeDtypeStruct((bsz, s, GDN_WIDTH), BF16)
    body = functools.partial(_gdn_kernel, reverse=reverse, n_blk=nb, n_ctx_blk=n_ctx_blk)
    if reverse:
        return pl.pallas_call(
            body, grid=(bsz, nb),
            in_specs=[tok_qkv] + gate_specs + par_specs,
            out_specs=tok_o, out_shape=o_shape, scratch_shapes=[state],
            compiler_params=_cparams(("parallel", "arbitrary")), name="gdn_bwd",
        )(qkv, gates, gates_t, gp, gpt)
    halo_specs = [
        pl.BlockSpec((1, HALO_ROWS, w3),
                     lambda b, i: (b, jnp.maximum(blk(i) * halo_per_blk - 1, 0), 0)),
        pl.BlockSpec((1, HALO_ROWS, w3),
                     lambda b, i: (b, jnp.minimum((blk(i) + 1) * halo_per_blk, n_halo - 1), 0))]
    return pl.pallas_call(
        body, grid=(bsz, nb),
        in_specs=[tok_qkv] + halo_specs + gate_specs + [pl.BlockSpec(conv_w.shape, lambda b, i: (0, 0))]
        + par_specs,
        out_specs=[tok_o, tok_qkv],
        out_shape=[o_shape, jax.ShapeDtypeStruct((bsz, s, w3), BF16)],
        scratch_shapes=[pltpu.VMEM((TOK_BLK + 16, w3), F32), state],
        compiler_params=_cparams(("parallel", "arbitrary")), name="gdn_fwd",
    )(qkv, qkv, qkv, gates, gates_t, conv_w, gp, gpt)


def _merge_kernel(*refs, alpha, n_experts):
    (x_ref, ug_ref, vg_ref, of_ref, ob_ref, zs_ref, sga_ref, sgb_ref, mods_ref,
     slg_ref, slb_ref, sw_ref, sbias_ref, nw_ref, wpa_ref, wpb_ref, wo_ref,
     l1g_ref, l1b_ref) = refs[:19]
    if n_experts:
        router_ref, x1_ref, h2_ref, rcol_ref, rrow_ref, cnt_ref = refs[19:]
    else:
        x1_ref, h2_ref = refs[19:]
    tb = TOK_BLK
    vn = (_norm(vg_ref[0].astype(F32)) * slg_ref[...] + slb_ref[...]).astype(BF16)
    gd = vn.shape[1] // SGU_GROUPS
    rows = []
    for c in range(tb // SGU_CHUNK):
        cols = []
        for g in range(SGU_GROUPS):
            cols.append(_dot(sw_ref[g], vn[c * SGU_CHUNK:(c + 1) * SGU_CHUNK, g * gd:(g + 1) * gd]))
        rows.append(jnp.concatenate(cols, axis=1) + sbias_ref[...])
    mixed = jnp.concatenate(rows, axis=0)
    oa = (ug_ref[0].astype(F32) * mixed).astype(BF16)
    osum = of_ref[0].astype(F32) + ob_ref[0].astype(F32)
    heads = []
    for h in range(GDN_HEADS):
        oh = osum[:, h * GDN_HEAD_DIM:(h + 1) * GDN_HEAD_DIM]
        heads.append(oh * lax.rsqrt(jnp.mean(oh * oh, axis=-1, keepdims=True) + LN_EPS) * nw_ref[...])
    ogd = (jnp.concatenate(heads, axis=1) * zs_ref[0].astype(F32)).astype(BF16)
    y = (sga_ref[0].astype(F32) * _dot(oa, wpa_ref[...])
         + sgb_ref[0].astype(F32) * _dot(ogd, wpb_ref[...]))
    t = _dot(y.astype(BF16), wo_ref[...])
    x1 = _norm(alpha * x_ref[0] + mods_ref[0, 0, 2:3, :] * t) * l1g_ref[...] + l1b_ref[...]
    x1_ref[0] = x1
    h2 = _norm(x1) * (1.0 + mods_ref[0, 0, 4:5, :]) + mods_ref[0, 0, 3:4, :]
    h2_ref[0] = h2.astype(BF16)
    if n_experts:
        logits = _dot3(h2, router_ref[...])
        lane = lax.broadcasted_iota(jnp.int32, logits.shape, 1).astype(F32)
        lg = jnp.where(lane < n_experts, logits, -jnp.inf)
        m1 = jnp.max(lg, axis=-1, keepdims=True)
        i1 = jnp.min(jnp.where(lg == m1, lane, float(LANES)), axis=-1, keepdims=True)
        lg2 = jnp.where(lane == i1, -jnp.inf, lg)
        m2 = jnp.max(lg2, axis=-1, keepdims=True)
        i2 = jnp.min(jnp.where(lg2 == m2, lane, float(LANES)), axis=-1, keepdims=True)
        e2 = jnp.exp(m2 - m1)
        den = 1.0 + e2
        gate = jnp.where(lane == i1, 1.0 / den, 0.0) + jnp.where(lane == i2, e2 / den, 0.0)
        sel = jnp.where(jnp.logical_or(lane == i1, lane == i2), 1.0, 0.0)
        selb = sel.astype(BF16)
        ti = lax.broadcasted_iota(jnp.int32, (tb, tb), 0)
        tj = lax.broadcasted_iota(jnp.int32, (tb, tb), 1)
        before = jnp.where(tj < ti, 1.0, 0.0).astype(BF16)
        rcol_ref[0] = jnp.where(sel > 0.0, _dot(before, selb), -1.0)
        er = lax.broadcasted_iota(jnp.int32, (8, LANES), 0)
        ec = lax.broadcasted_iota(jnp.int32, (8, LANES), 1)
        eye8 = jnp.where(er == ec, 1.0, 0.0).astype(BF16)
        sel_t = _dot_nt(eye8, selb)
        after = jnp.where(ti < tj, 1.0, 0.0).astype(BF16)
        rrow_ref[0, 0, 0:8, :] = jnp.where(sel_t > 0.0, _dot(sel_t.astype(BF16), after), -1.0)
        g_hi = gate.astype(BF16)
        g_mid = (gate - g_hi.astype(F32)).astype(BF16)
        g_lo = (gate - g_hi.astype(F32) - g_mid.astype(F32)).astype(BF16)
        rrow_ref[0, 0, 8:16, :] = _dot_nt(eye8, g_hi) + (_dot_nt(eye8, g_mid) + _dot_nt(eye8, g_lo))
        cnt_ref[0, 0] = _dot(jnp.ones((8, tb), BF16), selb)


def _merge(x, ug, vg, o_f, o_b, zs, sga, sgb, mods, sgu_ln_g, sgu_ln_b, sgu_w, sgu_bias, norm_w,
           w_pa, w_pb, w_o, ln_g, ln_b, n_ctx_blk, alpha, router=None):
    bsz, s, d = x.shape
    nb = s // TOK_BLK
    sw = d // 2
    n_experts = 0 if router is None else router[1]

    def tok(width):
        return pl.BlockSpec((1, TOK_BLK, width), lambda b, i: (b, i, 0))

    def full(a):
        return pl.BlockSpec(a.shape, lambda b, i: (0,) * a.ndim)

    consts = [sgu_ln_g.reshape(1, sw), sgu_ln_b.reshape(1, sw), sgu_w, sgu_bias,
              norm_w.reshape(1, GDN_HEAD_DIM), w_pa, w_pb, w_o, ln_g.reshape(1, d), ln_b.reshape(1, d)]
    if n_experts:
        consts.append(router[0])
    in_specs = [tok(d), tok(sw), tok(sw), tok(GDN_WIDTH), tok(GDN_WIDTH), tok(GDN_WIDTH), tok(d), tok(d),
                pl.BlockSpec((1, 1, 6, d), lambda b, i: (b, jnp.where(i < n_ctx_blk, 0, 1), 0, 0))]
    in_specs += [full(a) for a in consts]
    out_specs = [tok(d), tok(d)]
    out_shape = [jax.ShapeDtypeStruct((bsz, s, d), F32), jax.ShapeDtypeStruct((bsz, s, d), BF16)]
    if n_experts:
        assert n_experts <= 8
        out_specs += [tok(LANES),
                      pl.BlockSpec((1, 1, 16, TOK_BLK), lambda b, i: (b, i, 0, 0)),
                      pl.BlockSpec((1, 1, 8, LANES), lambda b, i: (b, i, 0, 0))]
        out_shape += [jax.ShapeDtypeStruct((bsz, s, LANES), F32),
                      jax.ShapeDtypeStruct((bsz, nb, 16, TOK_BLK), F32),
                      jax.ShapeDtypeStruct((bsz, nb, 8, LANES), F32)]
    return pl.pallas_call(
        functools.partial(_merge_kernel, alpha=alpha, n_experts=n_experts),
        grid=(bsz, nb),
        in_specs=in_specs,
        out_specs=out_specs,
        out_shape=out_shape,
        compiler_params=_cparams(("parallel", "arbitrary")),
        name="merge",
    )(x, ug, vg, o_f, o_b, zs, sga, sgb, mods, *consts)


def _ffn_kernel(h_ref, w1_ref, w3_ref, w2_ref, o_ref, acc_ref):
    f = pl.program_id(1)

    @pl.when(f == 0)
    def _():
        acc_ref[...] = jnp.zeros_like(acc_ref)

    h = h_ref[...]
    mid = (_silu(_dot(h, w1_ref[...])) * _dot(h, w3_ref[...])).astype(BF16)
    acc_ref[...] += _dot(mid, w2_ref[...])

    @pl.when(f == pl.num_programs(1) - 1)
    def _():
        o_ref[...] = acc_ref[...].astype(o_ref.dtype)


def _ffn(h, w1, w3, w2, tm, tf):
    t, d = h.shape
    ff = w1.shape[1]
    return pl.pallas_call(
        _ffn_kernel,
        grid=(t // tm, ff // tf),
        in_specs=[pl.BlockSpec((tm, d), lambda i, f: (i, 0)),
                  pl.BlockSpec((d, tf), lambda i, f: (0, f)),
                  pl.BlockSpec((d, tf), lambda i, f: (0, f)),
                  pl.BlockSpec((tf, d), lambda i, f: (f, 0))],
        out_specs=pl.BlockSpec((tm, d), lambda i, f: (i, 0)),
        out_shape=jax.ShapeDtypeStruct((t, d), BF16),
        scratch_shapes=[pltpu.VMEM((tm, d), F32)],
        compiler_params=_cparams(("parallel", "arbitrary")),
        name="ffn",
    )(h, w1, w3, w2)


MOE_GATHER_ROWS = 64
MOE_FFN_ROWS = 256
MOE_ROW_ALIGN = 16
MOE_TOKEN_TILE = 3072
FFN_TOKEN_TILE = 1536


def _moe_rows_capacity(tm):
    rows = tm + (tm // TOK_BLK) * MOE_ROW_ALIGN + TOK_BLK
    return -(-rows // MOE_FFN_ROWS) * MOE_FFN_ROWS


def _moe_kernel(cnt_ref, off_ref, tot_ref, h_ref, rcol_ref, rrow_ref, w1_ref, w3_ref, w2_ref,
                o_ref, xs_ref, ys_ref, wrow_ref, *, n_experts):
    i = pl.program_id(0)
    e = pl.program_id(1)
    f = pl.program_id(2)
    tm = h_ref.shape[0]
    n_sub = tm // TOK_BLK
    gr = MOE_GATHER_ROWS
    fr = MOE_FFN_ROWS
    half = fr // 2
    tot = tot_ref[i * n_experts + e]
    rem = tot - _blk_id(tot, fr) * fr
    short_tail = jnp.logical_and(rem > 0, rem <= half)
    n_full = _blk_id(tot, fr) + jnp.where(rem > half, 1, 0)

    def piece(j):
        base = (i * n_sub + j) * n_experts + e
        return pl.multiple_of(off_ref[base], MOE_ROW_ALIGN), cnt_ref[base]

    @pl.when(jnp.logical_and(e == 0, f == 0))
    def _():
        o_ref[...] = jnp.zeros_like(o_ref)
        xs_ref[...] = jnp.zeros_like(xs_ref)
        ys_ref[...] = jnp.zeros_like(ys_ref)
        wrow_ref[...] = jnp.zeros_like(wrow_ref)

    @pl.when(f == 0)
    def _():
        row_id = lax.broadcasted_iota(jnp.int32, (gr, TOK_BLK), 0).astype(F32)
        for j in range(n_sub):
            off, cnt = piece(j)
            rank = rrow_ref[j, pl.ds(e, 1), :]
            gate = rrow_ref[j, pl.ds(8 + e, 1), :]

            def gather(b, carry, j=j, off=off, rank=rank, gate=gate):
                hit = rank == row_id + (b * gr).astype(F32)
                rows = _dot(jnp.where(hit, 1.0, 0.0).astype(BF16), h_ref[j * TOK_BLK:(j + 1) * TOK_BLK, :])
                dst = pl.ds(pl.multiple_of(off + b * gr, MOE_ROW_ALIGN), gr)
                xs_ref[dst, :] = rows.astype(BF16)
                wgt = jnp.sum(jnp.where(hit, gate, 0.0), axis=-1, keepdims=True)
                wrow_ref[dst, :] = jnp.broadcast_to(wgt, (gr, LANES))
                return carry

            lax.fori_loop(0, _blk_id(cnt + (gr - 1), gr), gather, 0)

        def clear(rb, carry):
            ys_ref[pl.ds(pl.multiple_of(rb * fr, fr), fr), :] = jnp.zeros((fr, ys_ref.shape[1]), F32)
            return carry

        lax.fori_loop(0, _blk_id(tot + (fr - 1), fr), clear, 0)

    def ffn_rows(r0, n):
        x = xs_ref[pl.ds(r0, n), :]
        mid = _silu(_dot(x, w1_ref[0])) * _dot(x, w3_ref[0]) * wrow_ref[pl.ds(r0, n), :][:, 0:1]
        ys_ref[pl.ds(r0, n), :] += _dot(mid.astype(BF16), w2_ref[0])

    def ffn(rb, carry):
        ffn_rows(pl.multiple_of(rb * fr, fr), fr)
        return carry

    lax.fori_loop(0, n_full, ffn, 0)

    @pl.when(short_tail)
    def _():
        ffn_rows(pl.multiple_of(n_full * fr, fr), half)

    @pl.when(f == pl.num_programs(2) - 1)
    def _():
        col_id = lax.broadcasted_iota(jnp.int32, (TOK_BLK, TOK_BLK), 1).astype(F32)
        lane = lax.broadcasted_iota(jnp.int32, (TOK_BLK, LANES), 1)
        for j in range(n_sub):
            off, cnt = piece(j)

            @pl.when(cnt > 0)
            def _(j=j, off=off):
                tok = slice(j * TOK_BLK, (j + 1) * TOK_BLK)
                rank = jnp.sum(jnp.where(lane == e, rcol_ref[tok, :], 0.0), axis=-1, keepdims=True)
                onehot = jnp.where(rank == col_id, 1.0, 0.0).astype(BF16)
                o_ref[tok, :] += _dot(onehot, ys_ref[pl.ds(off, TOK_BLK), :].astype(BF16))


def _moe(h, rcol, rrow, cnt, off, tot, w1, w3, w2, tm, tf):
    t, d = h.shape
    ne, _, ff = w1.shape
    n_sub = tm // TOK_BLK
    cap = _moe_rows_capacity(tm)
    once = pl.Buffered(1)
    grid_spec = pltpu.PrefetchScalarGridSpec(
        num_scalar_prefetch=3,
        grid=(t // tm, ne, ff // tf),
        in_specs=[pl.BlockSpec((tm, d), lambda i, e, f, *_: (i, 0), pipeline_mode=once),
                  pl.BlockSpec((tm, LANES), lambda i, e, f, *_: (i, 0), pipeline_mode=once),
                  pl.BlockSpec((n_sub, 16, TOK_BLK), lambda i, e, f, *_: (i, 0, 0)),
                  pl.BlockSpec((1, d, tf), lambda i, e, f, *_: (e, 0, f)),
                  pl.BlockSpec((1, d, tf), lambda i, e, f, *_: (e, 0, f)),
                  pl.BlockSpec((1, tf, d), lambda i, e, f, *_: (e, f, 0))],
        out_specs=pl.BlockSpec((tm, d), lambda i, e, f, *_: (i, 0), pipeline_mode=once),
        scratch_shapes=[pltpu.VMEM((cap, d), BF16), pltpu.VMEM((cap, d), F32),
                        pltpu.VMEM((cap, LANES), F32)],
    )
    return pl.pallas_call(
        functools.partial(_moe_kernel, n_experts=ne),
        grid_spec=grid_spec,
        out_shape=jax.ShapeDtypeStruct((t, d), F32),
        compiler_params=_cparams(("parallel", "arbitrary", "arbitrary")),
        name="moe",
    )(cnt, off, tot, h, rcol, rrow, w1, w3, w2)


def _final_kernel(x_ref, f_ref, mods_ref, g_ref, b_ref, o_ref, *, alpha):
    x = alpha * x_ref[0] + mods_ref[0, 0, 5:6, :] * f_ref[0].astype(F32)
    o_ref[0] = _norm(x) * g_ref[...] + b_ref[...]


def _final(x1, f, mods, ln_g, ln_b, n_ctx_blk, alpha):
    bsz, s, d = x1.shape
    nb = s // TOK_BLK - n_ctx_blk
    return pl.pallas_call(
        functools.partial(_final_kernel, alpha=alpha),
        grid=(bsz, nb),
        in_specs=[pl.BlockSpec((1, TOK_BLK, d), lambda b, i: (b, i + n_ctx_blk, 0)),
                  pl.BlockSpec((1, TOK_BLK, d), lambda b, i: (b, i + n_ctx_blk, 0)),
                  pl.BlockSpec((1, 1, 6, d), lambda b, i: (b, 1, 0, 0)),
                  pl.BlockSpec((1, d), lambda b, i: (0, 0)),
                  pl.BlockSpec((1, d), lambda b, i: (0, 0))],
        out_specs=pl.BlockSpec((1, TOK_BLK, d), lambda b, i: (b, i, 0)),
        out_shape=jax.ShapeDtypeStruct((bsz, nb * TOK_BLK, d), F32),
        compiler_params=_cparams(("parallel", "arbitrary")),
        name="final",
    )(x1, f, mods, ln_g.reshape(1, d), ln_b.reshape(1, d))


def _token_tile(t, target):
    best = TOK_BLK
    for m in range(1, t // TOK_BLK + 1):
        cand = m * TOK_BLK
        if cand <= target and t % cand == 0:
            best = cand
    return best


def _ff_tile(ff, target):
    best = LANES
    for m in range(1, ff // LANES + 1):
        cand = m * LANES
        if cand <= target and ff % cand == 0:
            best = cand
    return best


def kernel(x, c, ctx, c_ctx, w_ada, b_ada, w_in, sgu_ln_g, sgu_ln_b, sgu_w, sgu_b, conv_w, a_log, dt_bias, gdn_norm_w, w_pa, w_pb, w_o, ln1_g, ln1_b, ln2_g, ln2_b, ffn_w1, ffn_w3, ffn_w2, moe_router, moe_w1, moe_w3, moe_w2):
    bsz, seq, d = x.shape
    ctx_len = ctx.shape[1]
    depth = w_ada.shape[0]
    alpha = (2.0 * depth) ** 0.25
    s = ctx_len + seq
    assert ctx_len % TOK_BLK == 0 and seq % TOK_BLK == 0
    assert d // 2 == SGU_GROUPS * LANES and bsz < 8
    n_ctx_blk = ctx_len // TOK_BLK
    sw = d // 2
    gw = GDN_WIDTH
    nh = GDN_HEADS

    xs = jnp.concatenate([ctx, x], axis=1)
    cond = jnp.zeros((8, d), F32).at[:bsz].set(c).at[bsz].set(c_ctx)

    t_all = bsz * s
    tm_ffn = _token_tile(t_all, FFN_TOKEN_TILE)

    f_prev = None
    prev = None
    x_cur = xs
    for l in range(depth):
        mod = _ada(cond, w_ada[l], b_ada[l]).reshape(8, 6, d)
        mods = jnp.stack([jnp.broadcast_to(mod[bsz], (bsz, 6, d)), mod[:bsz]], axis=1)

        g0 = 2 * sw + 4 * gw
        w_main = jnp.concatenate([w_in[l][:, :g0], w_in[l][:, g0 + 4 * nh:]], axis=1).astype(BF16)
        w_gate = jnp.pad(w_in[l][:, g0:g0 + 4 * nh], ((0, 0), (0, LANES - 4 * nh))).astype(BF16)
        outs = _inproj(x_cur, w_main, w_gate, mods, n_ctx_blk, prev=prev, alpha=alpha)
        if prev is not None:
            x_cur, outs = outs[0], outs[1:]
        ug, vg, qkv, zs, sga, sgb, gates = outs

        gates_t = jnp.swapaxes(gates[:, :, :4 * nh], 1, 2)
        al = jnp.zeros((LANES,), F32)
        db = jnp.zeros((LANES,), F32)
        for dd in range(2):
            al = al.at[2 * nh * dd:2 * nh * dd + nh].set(a_log[l, dd])
            db = db.at[2 * nh * dd:2 * nh * dd + nh].set(dt_bias[l, dd])
        gp = jnp.zeros((8, LANES), F32).at[0].set(al).at[1].set(db)
        gpt = jnp.stack([jnp.broadcast_to(al[:16, None], (16, LANES)),
                         jnp.broadcast_to(db[:16, None], (16, LANES))])
        cw = jnp.pad(conv_w[l], ((0, 8 - CONV_K), (0, 0)))
        o_f, qkvn = _gdn(qkv, gates, gates_t, cw, gp, gpt, n_ctx_blk, reverse=False)
        o_b = _gdn(qkvn, gates, gates_t, None, gp, gpt, n_ctx_blk, reverse=True)

        sgu_bias = jnp.repeat(sgu_b[l].T, LANES, axis=1)
        is_moe = l % 2 == 1
        router = None
        if is_moe:
            r = moe_router[l // 2]
            router = (jnp.pad(r, ((0, 0), (0, LANES - r.shape[1]))), r.shape[1])
        outs = _merge(x_cur, ug, vg, o_f, o_b, zs, sga, sgb, mods, sgu_ln_g[l], sgu_ln_b[l],
                      sgu_w[l].astype(BF16), sgu_bias, gdn_norm_w[l], w_pa[l].astype(BF16),
                      w_pb[l].astype(BF16), w_o[l].astype(BF16), ln1_g[l], ln1_b[l], n_ctx_blk, alpha,
                      router=router)
        x1, h2 = outs[0], outs[1]

        h2f = h2.reshape(t_all, d)
        if is_moe:
            ne = moe_router.shape[-1]
            rcol = outs[2].reshape(t_all, LANES)
            rrow = outs[3].reshape(t_all // TOK_BLK, 16, TOK_BLK)
            tm = _token_tile(t_all, MOE_TOKEN_TILE)
            n_sub = tm // TOK_BLK
            cnt = outs[4][:, :, 0, :ne].astype(jnp.int32).reshape(t_all // tm, n_sub, ne)
            aligned = (cnt + (MOE_ROW_ALIGN - 1)) // MOE_ROW_ALIGN * MOE_ROW_ALIGN
            off = jnp.cumsum(aligned, axis=1) - aligned
            tot = jnp.sum(aligned, axis=1)
            tf = _ff_tile(moe_w1.shape[-1], 512)
            f = _moe(h2f, rcol, rrow, cnt.reshape(-1), off.reshape(-1), tot.reshape(-1),
                     moe_w1[l // 2].astype(BF16), moe_w3[l // 2].astype(BF16),
                     moe_w2[l // 2].astype(BF16), tm, tf)
        else:
            tf = _ff_tile(ffn_w1.shape[-1], 512)
            f = _ffn(h2f, ffn_w1[l // 2].astype(BF16), ffn_w3[l // 2].astype(BF16),
                     ffn_w2[l // 2].astype(BF16), tm_ffn, tf)
        f = f.reshape(bsz, s, d)
        prev = (f, mods, ln2_g[l], ln2_b[l])
        f_prev, mods_prev = f, mods
        x_cur = x1

    return _final(x_cur, f_prev, mods_prev, ln2_g[depth - 1], ln2_b[depth - 1], n_ctx_blk, alpha)
```

```python
import functools
import math

import jax
import jax.numpy as jnp
from jax import lax
from jax.experimental import pallas as pl
from jax.experimental.pallas import tpu as pltpu

F32 = jnp.float32
BF16 = jnp.bfloat16

LN_EPS = 1e-6
SGU_CHUNK = 128
SGU_GROUPS = 4
GDN_HEADS = 4
GDN_HEAD_DIM = 128
GDN_WIDTH = GDN_HEADS * GDN_HEAD_DIM
GDN_CHUNK = 64
CONV_K = 5
TOP_K = 2

LANES = 128
TOK_BLK = 256
HALO_ROWS = 16
VMEM_LIMIT = 56 * 1024 * 1024


def _cparams(sem):
    return pltpu.CompilerParams(dimension_semantics=sem, vmem_limit_bytes=VMEM_LIMIT)


def _dot(a, b):
    return jnp.dot(a, b, preferred_element_type=F32)


def _dot_nt(a, b):
    return lax.dot_general(a, b, (((1,), (1,)), ((), ())), preferred_element_type=F32)


def _dot_tn(a, b):
    return lax.dot_general(a, b, (((0,), (0,)), ((), ())), preferred_element_type=F32)


def _split(a):
    hi = a.astype(BF16)
    lo = (a - hi.astype(F32)).astype(BF16)
    return hi, lo


def _dot3(a, b):
    ah, al = _split(a)
    bh, bl = _split(b)
    return _dot(ah, bh) + (_dot(ah, bl) + _dot(al, bh))


def _dot2_exact_lhs(a_bf, b):
    bh, bl = _split(b)
    return _dot(a_bf, bh) + _dot(a_bf, bl)


def _dot2_exact_rhs(a, b_bf):
    ah, al = _split(a)
    return _dot(ah, b_bf) + _dot(al, b_bf)


def _norm(x):
    mu = jnp.mean(x, axis=-1, keepdims=True)
    xc = x - mu
    var = jnp.mean(xc * xc, axis=-1, keepdims=True)
    return xc * lax.rsqrt(var + LN_EPS)


def _sigmoid(x):
    return 1.0 / (1.0 + jnp.exp(-x))


def _silu(x):
    return x * _sigmoid(x)


def _gelu(x):
    return 0.5 * x * (1.0 + lax.erf(x * (2.0 ** -0.5)))


def _blk_id(idx, size):
    return jnp.right_shift(idx, int(math.log2(size)))


def _softplus(x):
    return jnp.maximum(x, 0.0) + jnp.log1p(jnp.exp(-jnp.abs(x)))


def _ada_kernel(c_ref, w_ref, b_ref, o_ref):
    o_ref[...] = _dot3(_silu(c_ref[...]), w_ref[...]) + b_ref[...]


def _ada(cond, w, b):
    rows, d = cond.shape
    n = w.shape[1]
    tn = n // 6
    return pl.pallas_call(
        _ada_kernel,
        grid=(n // tn,),
        in_specs=[pl.BlockSpec((rows, d), lambda j: (0, 0)),
                  pl.BlockSpec((d, tn), lambda j: (0, j)),
                  pl.BlockSpec((1, tn), lambda j: (0, j))],
        out_specs=pl.BlockSpec((rows, tn), lambda j: (0, j)),
        out_shape=jax.ShapeDtypeStruct((rows, n), F32),
        compiler_params=_cparams(("arbitrary",)),
        name="ada",
    )(cond, w, b.reshape(1, n))


def _inproj_kernel(*refs, has_prev, alpha, seg, n_ctx_blk):
    if has_prev:
        (x_ref, f_ref, mprev_ref, lng_ref, lnb_ref, mods_ref, w_ref, wg_ref,
         xo_ref, ug_ref, vg_ref, qkv_ref, zs_ref, sga_ref, sgb_ref, gates_ref) = refs
        x = alpha * x_ref[0] + mprev_ref[0, 0, 5:6, :] * f_ref[0].astype(F32)
        x = _norm(x) * lng_ref[...] + lnb_ref[...]
        xo_ref[0] = x
    else:
        (c_ref, x_ref, mods_ref, w_ref, wg_ref,
         xo_ref, ug_ref, vg_ref, qkv_ref, zs_ref, sga_ref, sgb_ref, gates_ref) = refs
        x = jnp.where(pl.program_id(1) < n_ctx_blk, c_ref[0], x_ref[0])
        xo_ref[0] = x
    h = _norm(x) * (1.0 + mods_ref[0, 0, 1:2, :]) + mods_ref[0, 0, 0:1, :]
    hb = h.astype(BF16)

    def proj(lo, hi):
        return _dot(hb, w_ref[:, lo:hi])

    sw, gw = seg
    c0 = 0
    ug_ref[0] = _gelu(proj(c0, c0 + sw)).astype(BF16)
    c0 += sw
    vg_ref[0] = _gelu(proj(c0, c0 + sw)).astype(BF16)
    c0 += sw
    for j in range(3):
        qkv_ref[0, :, j * gw:(j + 1) * gw] = proj(c0, c0 + gw).astype(BF16)
        c0 += gw
    zs_ref[0] = _silu(proj(c0, c0 + gw)).astype(BF16)
    c0 += gw
    d = sga_ref.shape[-1]
    half = d // 2
    for j in range(2):
        sga_ref[0, :, j * half:(j + 1) * half] = _sigmoid(proj(c0, c0 + half)).astype(BF16)
        c0 += half
    for j in range(2):
        sgb_ref[0, :, j * half:(j + 1) * half] = _sigmoid(proj(c0, c0 + half)).astype(BF16)
        c0 += half
    gates_ref[0] = _dot(hb, wg_ref[...])


def _inproj(x, w_main, w_gate, mods, n_ctx_blk, prev=None, ctx=None, alpha=1.0):
    bsz, s, d = x.shape
    if ctx is not None:
        s += ctx.shape[1]
    nb = s // TOK_BLK
    sw = d // 2
    gw = GDN_WIDTH
    n_main = w_main.shape[1]

    def tok(width):
        return pl.BlockSpec((1, TOK_BLK, width), lambda b, i: (b, i, 0))

    def mod_spec():
        return pl.BlockSpec((1, 1, 6, d), lambda b, i: (b, jnp.where(i < n_ctx_blk, 0, 1), 0, 0))

    def full(shape):
        return pl.BlockSpec(shape, lambda b, i: (0,) * len(shape))

    if ctx is not None:
        in_specs = [pl.BlockSpec((1, TOK_BLK, d), lambda b, i: (b, jnp.minimum(i, n_ctx_blk - 1), 0)),
                    pl.BlockSpec((1, TOK_BLK, d), lambda b, i: (b, jnp.maximum(i - n_ctx_blk, 0), 0))]
        args = [ctx, x]
    else:
        in_specs = [tok(d)]
        args = [x]
    if prev is not None:
        f_prev, mods_prev, ln_g, ln_b = prev
        in_specs += [tok(d), mod_spec(), full((1, d)), full((1, d))]
        args += [f_prev, mods_prev, ln_g.reshape(1, d), ln_b.reshape(1, d)]
    in_specs += [mod_spec(), full((d, n_main)), full((d, LANES))]
    args += [mods, w_main, w_gate]

    out_specs = []
    out_shape = []
    out_specs.append(tok(d))
    out_shape.append(jax.ShapeDtypeStruct((bsz, s, d), F32))
    for width, dt in ((sw, BF16), (sw, BF16), (3 * gw, BF16), (gw, BF16), (d, BF16), (d, BF16),
                      (LANES, F32)):
        out_specs.append(tok(width))
        out_shape.append(jax.ShapeDtypeStruct((bsz, s, width), dt))

    return pl.pallas_call(
        functools.partial(_inproj_kernel, has_prev=prev is not None, alpha=alpha, seg=(sw, gw),
                          n_ctx_blk=n_ctx_blk),
        grid=(bsz, nb),
        in_specs=in_specs,
        out_specs=out_specs,
        out_shape=out_shape,
        compiler_params=_cparams(("parallel", "arbitrary")),
        name="inproj",
    )(*args)


def _gdn_block(i, n_blk, n_ctx_blk, reverse):
    if not reverse:
        return i
    return jnp.where(i < n_ctx_blk, n_ctx_blk - 1 - i, n_blk - 1 - (i - n_ctx_blk))


def _gdn_kernel(*refs, reverse, n_blk, n_ctx_blk):
    if reverse:
        qkvn_ref, g_ref, gt_ref, gp_ref, gpt_ref, o_ref, s_ref = refs
    else:
        (qkv_ref, prev_ref, next_ref, g_ref, gt_ref, cw_ref, gp_ref, gpt_ref,
         o_ref, qkvn_ref, win_ref, s_ref) = refs
    i = pl.program_id(1)
    blk = _gdn_block(i, n_blk, n_ctx_blk, reverse)
    tb = TOK_BLK
    hd = GDN_HEAD_DIM
    nh = GDN_HEADS
    d_idx = 1 if reverse else 0

    @pl.when(i == 0)
    def _():
        s_ref[...] = jnp.zeros_like(s_ref)

    if not reverse:
        first = jnp.logical_or(blk == 0, blk == n_ctx_blk)
        last = jnp.logical_or(blk == n_ctx_blk - 1, blk == n_blk - 1)
        prev = prev_ref[0].astype(F32)[HALO_ROWS - 8:, :]
        nxt = next_ref[0].astype(F32)[:8, :]
        win_ref[0:8, :] = jnp.where(first, 0.0, prev)
        win_ref[8:8 + tb, :] = qkv_ref[0].astype(F32)
        win_ref[8 + tb:16 + tb, :] = jnp.where(last, 0.0, nxt)

    def conv_slab(col):
        acc = None
        for j in range(CONV_K):
            off = 8 - CONV_K // 2 + j
            term = cw_ref[j:j + 1, col:col + hd] * win_ref[off:off + tb, col:col + hd]
            acc = term if acc is None else acc + term
        return _silu(acc)

    def l2n(x):
        return x * lax.rsqrt(jnp.sum(x * x, axis=-1, keepdims=True) + LN_EPS)

    def qkv_slab(col, kind):
        if reverse:
            return qkvn_ref[0, :, col:col + hd].astype(F32)
        x = conv_slab(col)
        if kind == 0:
            x = l2n(x) * (hd ** -0.5)
        elif kind == 1:
            x = l2n(x)
        qkvn_ref[0, :, col:col + hd] = x.astype(BF16)
        return x

    g = g_ref[0]
    lg_c = -jnp.exp(gp_ref[0:1, :]) * _softplus(g + gp_ref[1:2, :])
    beta_c = _sigmoid(g)
    gt = gt_ref[0]
    lg_r = -jnp.exp(gpt_ref[0][:, 0:1]) * _softplus(gt + gpt_ref[1][:, 0:1])

    ii = lax.broadcasted_iota(jnp.int32, (tb, tb), 0)
    jj = lax.broadcasted_iota(jnp.int32, (tb, tb), 1)
    same_chunk = _blk_id(ii, GDN_CHUNK) == _blk_id(jj, GDN_CHUNK)
    cum = jnp.logical_and(same_chunk, (ii <= jj) if reverse else (ii >= jj))
    cum_c = jnp.where(cum, 1.0, 0.0).astype(BF16)
    cum_r = jnp.where(jnp.logical_and(same_chunk, (jj <= ii) if reverse else (jj >= ii)),
                      1.0, 0.0).astype(BF16)
    ones_c = jnp.where(same_chunk, 1.0, 0.0).astype(BF16)
    gam_c = _dot2_exact_lhs(cum_c, lg_c)
    tot_c = _dot2_exact_lhs(ones_c, lg_c)
    gam_r = _dot2_exact_rhs(lg_r, cum_r)

    sc_rows = 2 * GDN_CHUNK
    pi = lax.broadcasted_iota(jnp.int32, (sc_rows, sc_rows), 0)
    pj = lax.broadcasted_iota(jnp.int32, (sc_rows, sc_rows), 1)
    tri_incl = (pi <= pj) if reverse else (pi >= pj)
    tri_strict = (pi < pj) if reverse else (pi > pj)

    def same(level):
        return _blk_id(pi, level) == _blk_id(pj, level)

    incl = jnp.logical_and(same(GDN_CHUNK), tri_incl)
    strict = jnp.logical_and(same(GDN_CHUNK), tri_strict)
    m16 = jnp.logical_and(strict, same(16))
    m32 = jnp.logical_and(jnp.logical_and(strict, same(32)), jnp.logical_not(same(16)))
    m64 = jnp.logical_and(strict, jnp.logical_not(same(32)))
    eye = jnp.where(pi == pj, 1.0, 0.0)

    n_sc = tb // sc_rows
    sc_order = range(n_sc - 1, -1, -1) if reverse else range(n_sc)
    ch_order = (1, 0) if reverse else (0, 1)

    heads = range(nh)
    tiles = [(h, sc) for h in heads for sc in range(n_sc)]
    q_all = [qkv_slab(h * hd, 0) for h in heads]
    k_all = [qkv_slab(GDN_WIDTH + h * hd, 1) for h in heads]
    v_all = [qkv_slab(2 * GDN_WIDTH + h * hd, 2) for h in heads]

    def rows(a, sc):
        return a[sc * sc_rows:(sc + 1) * sc_rows]

    def col_a(h):
        return 2 * nh * d_idx + h

    q_t = {t: rows(q_all[t[0]], t[1]) for t in tiles}
    k_t = {t: rows(k_all[t[0]], t[1]) for t in tiles}
    v_t = {t: rows(v_all[t[0]], t[1]) for t in tiles}
    gcol = {(h, sc): rows(gam_c, sc)[:, col_a(h):col_a(h) + 1] for h, sc in tiles}
    tcol = {(h, sc): rows(tot_c, sc)[:, col_a(h):col_a(h) + 1] for h, sc in tiles}
    bcol = {(h, sc): rows(beta_c, sc)[:, col_a(h) + nh:col_a(h) + nh + 1] for h, sc in tiles}
    grow = {(h, sc): gam_r[col_a(h):col_a(h) + 1, sc * sc_rows:(sc + 1) * sc_rows] for h, sc in tiles}
    decay = {t: jnp.exp(jnp.where(incl, gcol[t] - grow[t], -jnp.inf)) for t in tiles}
    eg = {t: jnp.exp(gcol[t]) for t in tiles}
    kb = {t: k_t[t].astype(BF16) for t in tiles}
    qb = {t: q_t[t].astype(BF16) for t in tiles}
    kk = {t: _dot_nt(kb[t], kb[t]) for t in tiles}
    qk = {t: (_dot_nt(qb[t], kb[t]) * decay[t]).astype(BF16) for t in tiles}
    neg_l = {t: jnp.where(strict, -(bcol[t] * kk[t] * decay[t]), 0.0) for t in tiles}
    x1 = {t: jnp.where(m16, neg_l[t], 0.0).astype(BF16) for t in tiles}
    p = {t: eye + x1[t].astype(F32) for t in tiles}
    xp = x1
    for _ in range(3):
        xp = {t: _dot(xp[t], xp[t]).astype(BF16) for t in tiles}
        p = {t: p[t] + _dot(p[t].astype(BF16), xp[t]) for t in tiles}
    for mask in (m32, m64):
        pb = {t: p[t].astype(BF16) for t in tiles}
        pe = {t: _dot(pb[t], jnp.where(mask, neg_l[t], 0.0).astype(BF16)).astype(BF16) for t in tiles}
        p = {t: p[t] + _dot(pe[t], pb[t]) for t in tiles}
    rhs = {t: jnp.concatenate([bcol[t] * v_t[t], (bcol[t] * eg[t]) * k_t[t]], axis=1).astype(BF16)
           for t in tiles}
    sol = {t: _dot(p[t].astype(BF16), rhs[t]) for t in tiles}
    kd = {t: (k_t[t] * jnp.exp(tcol[t] - gcol[t])).astype(BF16) for t in tiles}
    wkqg = {t: jnp.concatenate([sol[t][:, hd:], q_t[t] * eg[t]], axis=0).astype(BF16) for t in tiles}

    s = [s_ref[h] for h in heads]
    for sc in sc_order:
        w_parts = [[None, None] for _ in heads]
        og_parts = [[None, None] for _ in heads]
        for ch in ch_order:
            c0 = ch * GDN_CHUNK
            for h in heads:
                t = (h, sc)
                lhs = jnp.concatenate([wkqg[t][c0:c0 + GDN_CHUNK],
                                       wkqg[t][sc_rows + c0:sc_rows + c0 + GDN_CHUNK]], axis=0)
                prod = _dot(lhs, s[h].astype(BF16))
                w_ch = sol[t][c0:c0 + GDN_CHUNK, :hd] - prod[:GDN_CHUNK]
                og_parts[h][ch] = prod[GDN_CHUNK:]
                w_parts[h][ch] = w_ch
                gc = jnp.exp(tcol[t][c0:c0 + 1, :])
                s[h] = gc * s[h] + _dot_tn(kd[t][c0:c0 + GDN_CHUNK], w_ch.astype(BF16))
        for h in heads:
            w_full = jnp.concatenate(w_parts[h], axis=0).astype(BF16)
            o = jnp.concatenate(og_parts[h], axis=0) + _dot(qk[(h, sc)], w_full)
            o_ref[0, sc * sc_rows:(sc + 1) * sc_rows, h * hd:(h + 1) * hd] = o.astype(o_ref.dtype)
    for h in heads:
        s_ref[h] = s[h]


def _gdn(qkv, gates, gates_t, conv_w, gp, gpt, n_ctx_blk, reverse):
    bsz, s, w3 = qkv.shape
    nb = s // TOK_BLK
    halo_per_blk = TOK_BLK // HALO_ROWS
    n_halo = s // HALO_ROWS

    def blk(i):
        return _gdn_block(i, nb, n_ctx_blk, reverse)

    tok_qkv = pl.BlockSpec((1, TOK_BLK, w3), lambda b, i: (b, blk(i), 0))
    tok_o = pl.BlockSpec((1, TOK_BLK, GDN_WIDTH), lambda b, i: (b, blk(i), 0))
    gate_specs = [pl.BlockSpec((1, TOK_BLK, LANES), lambda b, i: (b, blk(i), 0)),
                  pl.BlockSpec((1, 16, TOK_BLK), lambda b, i: (b, 0, blk(i)))]
    par_specs = [pl.BlockSpec(gp.shape, lambda b, i: (0, 0)),
                 pl.BlockSpec(gpt.shape, lambda b, i: (0, 0, 0))]
    state = pltpu.VMEM((GDN_HEADS, GDN_HEAD_DIM, GDN_HEAD_DIM), F32)
    o_shape = jax.ShapeDtypeStruct((bsz, s, GDN_WIDTH), BF16)
    body = functools.partial(_gdn_kernel, reverse=reverse, n_blk=nb, n_ctx_blk=n_ctx_blk)
    if reverse:
        return pl.pallas_call(
            body, grid=(bsz, nb),
            in_specs=[tok_qkv] + gate_specs + par_specs,
            out_specs=tok_o, out_shape=o_shape, scratch_shapes=[state],
            compiler_params=_cparams(("parallel", "arbitrary")), name="gdn_bwd",
        )(qkv, gates, gates_t, gp, gpt)
    halo_specs = [
        pl.BlockSpec((1, HALO_ROWS, w3),
                     lambda b, i: (b, jnp.maximum(blk(i) * halo_per_blk - 1, 0), 0)),
        pl.BlockSpec((1, HALO_ROWS, w3),
                     lambda b, i: (b, jnp.minimum((blk(i) + 1) * halo_per_blk, n_halo - 1), 0))]
    return pl.pallas_call(
        body, grid=(bsz, nb),
        in_specs=[tok_qkv] + halo_specs + gate_specs + [pl.BlockSpec(conv_w.shape, lambda b, i: (0, 0))]
        + par_specs,
        out_specs=[tok_o, tok_qkv],
        out_shape=[o_shape, jax.ShapeDtypeStruct((bsz, s, w3), BF16)],
        scratch_shapes=[pltpu.VMEM((TOK_BLK + 16, w3), F32), state],
        compiler_params=_cparams(("parallel", "arbitrary")), name="gdn_fwd",
    )(qkv, qkv, qkv, gates, gates_t, conv_w, gp, gpt)


def _merge_kernel(*refs, alpha, n_experts):
    (x_ref, ug_ref, vg_ref, of_ref, ob_ref, zs_ref, sga_ref, sgb_ref, mods_ref,
     slg_ref, slb_ref, sw_ref, sbias_ref, nw_ref, wpa_ref, wpb_ref, wo_ref,
     l1g_ref, l1b_ref) = refs[:19]
    if n_experts:
        router_ref, x1_ref, h2_ref, rcol_ref, rrow_ref, cnt_ref = refs[19:]
    else:
        x1_ref, h2_ref = refs[19:]
    tb = TOK_BLK
    vn = (_norm(vg_ref[0].astype(F32)) * slg_ref[...] + slb_ref[...]).astype(BF16)
    gd = vn.shape[1] // SGU_GROUPS
    rows = []
    for c in range(tb // SGU_CHUNK):
        cols = []
        for g in range(SGU_GROUPS):
            cols.append(_dot(sw_ref[g], vn[c * SGU_CHUNK:(c + 1) * SGU_CHUNK, g * gd:(g + 1) * gd]))
        rows.append(jnp.concatenate(cols, axis=1) + sbias_ref[...])
    mixed = jnp.concatenate(rows, axis=0)
    oa = (ug_ref[0].astype(F32) * mixed).astype(BF16)
    osum = of_ref[0].astype(F32) + ob_ref[0].astype(F32)
    heads = []
    for h in range(GDN_HEADS):
        oh = osum[:, h * GDN_HEAD_DIM:(h + 1) * GDN_HEAD_DIM]
        heads.append(oh * lax.rsqrt(jnp.mean(oh * oh, axis=-1, keepdims=True) + LN_EPS) * nw_ref[...])
    ogd = (jnp.concatenate(heads, axis=1) * zs_ref[0].astype(F32)).astype(BF16)
    y = (sga_ref[0].astype(F32) * _dot(oa, wpa_ref[...])
         + sgb_ref[0].astype(F32) * _dot(ogd, wpb_ref[...]))
    t = _dot(y.astype(BF16), wo_ref[...])
    x1 = _norm(alpha * x_ref[0] + mods_ref[0, 0, 2:3, :] * t) * l1g_ref[...] + l1b_ref[...]
    x1_ref[0] = x1
    h2 = _norm(x1) * (1.0 + mods_ref[0, 0, 4:5, :]) + mods_ref[0, 0, 3:4, :]
    h2_ref[0] = h2.astype(BF16)
    if n_experts:
        logits = _dot3(h2, router_ref[...])
        lane = lax.broadcasted_iota(jnp.int32, logits.shape, 1).astype(F32)
        lg = jnp.where(lane < n_experts, logits, -jnp.inf)
        m1 = jnp.max(lg, axis=-1, keepdims=True)
        i1 = jnp.min(jnp.where(lg == m1, lane, float(LANES)), axis=-1, keepdims=True)
        lg2 = jnp.where(lane == i1, -jnp.inf, lg)
        m2 = jnp.max(lg2, axis=-1, keepdims=True)
        i2 = jnp.min(jnp.where(lg2 == m2, lane, float(LANES)), axis=-1, keepdims=True)
        e2 = jnp.exp(m2 - m1)
        den = 1.0 + e2
        gate = jnp.where(lane == i1, 1.0 / den, 0.0) + jnp.where(lane == i2, e2 / den, 0.0)
        sel = jnp.where(jnp.logical_or(lane == i1, lane == i2), 1.0, 0.0)
        selb = sel.astype(BF16)
        ti = lax.broadcasted_iota(jnp.int32, (tb, tb), 0)
        tj = lax.broadcasted_iota(jnp.int32, (tb, tb), 1)
        before = jnp.where(tj < ti, 1.0, 0.0).astype(BF16)
        rcol_ref[0] = jnp.where(sel > 0.0, _dot(before, selb), MOE_NOT_SELECTED)
        er = lax.broadcasted_iota(jnp.int32, (8, LANES), 0)
        ec = lax.broadcasted_iota(jnp.int32, (8, LANES), 1)
        eye8 = jnp.where(er == ec, 1.0, 0.0).astype(BF16)
        sel_t = _dot_nt(eye8, selb)
        after = jnp.where(ti < tj, 1.0, 0.0).astype(BF16)
        rrow_ref[0, 0, 0:8, :] = jnp.where(sel_t > 0.0, _dot(sel_t.astype(BF16), after), MOE_NOT_SELECTED)
        g_hi = gate.astype(BF16)
        g_mid = (gate - g_hi.astype(F32)).astype(BF16)
        g_lo = (gate - g_hi.astype(F32) - g_mid.astype(F32)).astype(BF16)
        rrow_ref[0, 0, 8:16, :] = _dot_nt(eye8, g_hi) + (_dot_nt(eye8, g_mid) + _dot_nt(eye8, g_lo))
        cnt_ref[0, 0] = _dot(jnp.ones((8, tb), BF16), selb)


def _merge(x, ug, vg, o_f, o_b, zs, sga, sgb, mods, sgu_ln_g, sgu_ln_b, sgu_w, sgu_bias, norm_w,
           w_pa, w_pb, w_o, ln_g, ln_b, n_ctx_blk, alpha, router=None):
    bsz, s, d = x.shape
    nb = s // TOK_BLK
    sw = d // 2
    n_experts = 0 if router is None else router[1]

    def tok(width):
        return pl.BlockSpec((1, TOK_BLK, width), lambda b, i: (b, i, 0))

    def full(a):
        return pl.BlockSpec(a.shape, lambda b, i: (0,) * a.ndim)

    consts = [sgu_ln_g.reshape(1, sw), sgu_ln_b.reshape(1, sw), sgu_w, sgu_bias,
              norm_w.reshape(1, GDN_HEAD_DIM), w_pa, w_pb, w_o, ln_g.reshape(1, d), ln_b.reshape(1, d)]
    if n_experts:
        consts.append(router[0])
    in_specs = [tok(d), tok(sw), tok(sw), tok(GDN_WIDTH), tok(GDN_WIDTH), tok(GDN_WIDTH), tok(d), tok(d),
                pl.BlockSpec((1, 1, 6, d), lambda b, i: (b, jnp.where(i < n_ctx_blk, 0, 1), 0, 0))]
    in_specs += [full(a) for a in consts]
    out_specs = [tok(d), tok(d)]
    out_shape = [jax.ShapeDtypeStruct((bsz, s, d), F32), jax.ShapeDtypeStruct((bsz, s, d), BF16)]
    if n_experts:
        assert n_experts <= 8
        out_specs += [tok(LANES),
                      pl.BlockSpec((1, 1, 16, TOK_BLK), lambda b, i: (b, i, 0, 0)),
                      pl.BlockSpec((1, 1, 8, LANES), lambda b, i: (b, i, 0, 0))]
        out_shape += [jax.ShapeDtypeStruct((bsz, s, LANES), F32),
                      jax.ShapeDtypeStruct((bsz, nb, 16, TOK_BLK), F32),
                      jax.ShapeDtypeStruct((bsz, nb, 8, LANES), F32)]
    return pl.pallas_call(
        functools.partial(_merge_kernel, alpha=alpha, n_experts=n_experts),
        grid=(bsz, nb),
        in_specs=in_specs,
        out_specs=out_specs,
        out_shape=out_shape,
        compiler_params=_cparams(("parallel", "arbitrary")),
        name="merge",
    )(x, ug, vg, o_f, o_b, zs, sga, sgb, mods, *consts)


def _ffn_kernel(h_ref, w1_ref, w3_ref, w2_ref, o_ref, acc_ref):
    f = pl.program_id(1)

    @pl.when(f == 0)
    def _():
        acc_ref[...] = jnp.zeros_like(acc_ref)

    h = h_ref[...]
    mid = (_silu(_dot(h, w1_ref[...])) * _dot(h, w3_ref[...])).astype(BF16)
    acc_ref[...] += _dot(mid, w2_ref[...])

    @pl.when(f == pl.num_programs(1) - 1)
    def _():
        o_ref[...] = acc_ref[...].astype(o_ref.dtype)


def _ffn(h, w1, w3, w2, tm, tf):
    t, d = h.shape
    ff = w1.shape[1]
    return pl.pallas_call(
        _ffn_kernel,
        grid=(t // tm, ff // tf),
        in_specs=[pl.BlockSpec((tm, d), lambda i, f: (i, 0)),
                  pl.BlockSpec((d, tf), lambda i, f: (0, f)),
                  pl.BlockSpec((d, tf), lambda i, f: (0, f)),
                  pl.BlockSpec((tf, d), lambda i, f: (f, 0))],
        out_specs=pl.BlockSpec((tm, d), lambda i, f: (i, 0)),
        out_shape=jax.ShapeDtypeStruct((t, d), BF16),
        scratch_shapes=[pltpu.VMEM((tm, d), F32)],
        compiler_params=_cparams(("parallel", "arbitrary")),
        name="ffn",
    )(h, w1, w3, w2)


MOE_GATHER_ROWS = 128
MOE_FFN_ROWS = 256
MOE_ROW_ALIGN = 16
MOE_NOT_SELECTED = -4096.0
MOE_TOKEN_TILE = 3072
FFN_TOKEN_TILE = 1536


def _moe_rows_capacity(tm):
    rows = tm + TOK_BLK + LANES
    return -(-rows // MOE_FFN_ROWS) * MOE_FFN_ROWS


def _moe_kernel(cnt_ref, off_ref, tot_ref, h_ref, rcol_ref, rrow_ref, w1_ref, w3_ref, w2_ref,
                o_ref, xs_ref, ys_ref, wrow_ref, *, n_experts):
    i = pl.program_id(0)
    e = pl.program_id(1)
    f = pl.program_id(2)
    tm = h_ref.shape[0]
    n_sub = tm // TOK_BLK
    gr = MOE_GATHER_ROWS
    fr = MOE_FFN_ROWS
    half = fr // 2
    tot = tot_ref[i * n_experts + e]
    rem = tot - _blk_id(tot, fr) * fr
    short_tail = jnp.logical_and(rem > 0, rem <= half)
    n_full = _blk_id(tot, fr) + jnp.where(rem > half, 1, 0)

    def piece(j):
        base = (i * n_sub + j) * n_experts + e
        off = off_ref[base]
        start = pl.multiple_of(_blk_id(off, MOE_ROW_ALIGN) * MOE_ROW_ALIGN, MOE_ROW_ALIGN)
        return start, off - start, cnt_ref[base]

    @pl.when(jnp.logical_and(e == 0, f == 0))
    def _():
        o_ref[...] = jnp.zeros_like(o_ref)
        xs_ref[...] = jnp.zeros_like(xs_ref)
        ys_ref[...] = jnp.zeros_like(ys_ref)
        wrow_ref[...] = jnp.zeros_like(wrow_ref)

    @pl.when(f == 0)
    def _():
        def clear(rb, carry):
            rows = pl.ds(pl.multiple_of(rb * fr, fr), fr)
            xs_ref[rows, :] = jnp.zeros((fr, xs_ref.shape[1]), BF16)
            ys_ref[rows, :] = jnp.zeros((fr, ys_ref.shape[1]), F32)
            wrow_ref[rows, :] = jnp.zeros((fr, LANES), F32)
            return carry

        lax.fori_loop(0, _blk_id(tot + (fr - 1), fr) + 1, clear, 0)

        row_id = lax.broadcasted_iota(jnp.int32, (gr, TOK_BLK), 0).astype(F32)
        for j in range(n_sub):
            start, shift, cnt = piece(j)
            rank = rrow_ref[j, pl.ds(e, 1), :]
            gate = rrow_ref[j, pl.ds(8 + e, 1), :]
            row0 = row_id - shift.astype(F32)

            def gather(b, carry, j=j, start=start, rank=rank, gate=gate, row0=row0):
                hit = rank == row0 + (b * gr).astype(F32)
                rows = _dot(jnp.where(hit, 1.0, 0.0).astype(BF16), h_ref[j * TOK_BLK:(j + 1) * TOK_BLK, :])
                dst = pl.ds(pl.multiple_of(start + b * gr, MOE_ROW_ALIGN), gr)
                xs_ref[dst, :] = xs_ref[dst, :] + rows.astype(BF16)
                wgt = jnp.sum(jnp.where(hit, gate, 0.0), axis=-1, keepdims=True)
                wrow_ref[dst, :] = wrow_ref[dst, :] + jnp.broadcast_to(wgt, (gr, LANES))
                return carry

            lax.fori_loop(0, _blk_id(shift + cnt + (gr - 1), gr), gather, 0)

    def ffn_rows(r0, n):
        x = xs_ref[pl.ds(r0, n), :]
        mid = _silu(_dot(x, w1_ref[0])) * _dot(x, w3_ref[0]) * wrow_ref[pl.ds(r0, n), :][:, 0:1]
        ys_ref[pl.ds(r0, n), :] += _dot(mid.astype(BF16), w2_ref[0])

    def ffn(rb, carry):
        ffn_rows(pl.multiple_of(rb * fr, fr), fr)
        return carry

    lax.fori_loop(0, n_full, ffn, 0)

    @pl.when(short_tail)
    def _():
        ffn_rows(pl.multiple_of(n_full * fr, fr), half)

    @pl.when(f == pl.num_programs(2) - 1)
    def _():
        col_id = lax.broadcasted_iota(jnp.int32, (TOK_BLK, LANES), 1).astype(F32)
        lane = lax.broadcasted_iota(jnp.int32, (TOK_BLK, LANES), 1)
        for j in range(n_sub):
            start, shift, cnt = piece(j)
            tok = slice(j * TOK_BLK, (j + 1) * TOK_BLK)

            def rank_col(tok=tok):
                return jnp.sum(jnp.where(lane == e, rcol_ref[tok, :], 0.0), axis=-1, keepdims=True)

            for w in range(TOK_BLK // LANES + 1):

                @pl.when(jnp.logical_and(cnt > 0, shift + cnt > w * LANES))
                def _(w=w, tok=tok, start=start, shift=shift, rank_col=rank_col):
                    cols = col_id - (shift - w * LANES).astype(F32)
                    onehot = jnp.where(rank_col() == cols, 1.0, 0.0).astype(BF16)
                    rows = pl.ds(pl.multiple_of(start + w * LANES, MOE_ROW_ALIGN), LANES)
                    o_ref[tok, :] += _dot(onehot, ys_ref[rows, :].astype(BF16))


def _moe(h, rcol, rrow, cnt, off, tot, w1, w3, w2, tm, tf):
    t, d = h.shape
    ne, _, ff = w1.shape
    n_sub = tm // TOK_BLK
    cap = _moe_rows_capacity(tm)
    once = pl.Buffered(1)
    grid_spec = pltpu.PrefetchScalarGridSpec(
        num_scalar_prefetch=3,
        grid=(t // tm, ne, ff // tf),
        in_specs=[pl.BlockSpec((tm, d), lambda i, e, f, *_: (i, 0), pipeline_mode=once),
                  pl.BlockSpec((tm, LANES), lambda i, e, f, *_: (i, 0), pipeline_mode=once),
                  pl.BlockSpec((n_sub, 16, TOK_BLK), lambda i, e, f, *_: (i, 0, 0)),
                  pl.BlockSpec((1, d, tf), lambda i, e, f, *_: (e, 0, f)),
                  pl.BlockSpec((1, d, tf), lambda i, e, f, *_: (e, 0, f)),
                  pl.BlockSpec((1, tf, d), lambda i, e, f, *_: (e, f, 0))],
        out_specs=pl.BlockSpec((tm, d), lambda i, e, f, *_: (i, 0), pipeline_mode=once),
        scratch_shapes=[pltpu.VMEM((cap, d), BF16), pltpu.VMEM((cap, d), F32),
                        pltpu.VMEM((cap, LANES), F32)],
    )
    return pl.pallas_call(
        functools.partial(_moe_kernel, n_experts=ne),
        grid_spec=grid_spec,
        out_shape=jax.ShapeDtypeStruct((t, d), F32),
        compiler_params=_cparams(("parallel", "arbitrary", "arbitrary")),
        name="moe",
    )(cnt, off, tot, h, rcol, rrow, w1, w3, w2)


def _final_kernel(x_ref, f_ref, mods_ref, g_ref, b_ref, o_ref, *, alpha):
    x = alpha * x_ref[0] + mods_ref[0, 0, 5:6, :] * f_ref[0].astype(F32)
    o_ref[0] = _norm(x) * g_ref[...] + b_ref[...]


def _final(x1, f, mods, ln_g, ln_b, n_ctx_blk, alpha):
    bsz, s, d = x1.shape
    nb = s // TOK_BLK - n_ctx_blk
    return pl.pallas_call(
        functools.partial(_final_kernel, alpha=alpha),
        grid=(bsz, nb),
        in_specs=[pl.BlockSpec((1, TOK_BLK, d), lambda b, i: (b, i + n_ctx_blk, 0)),
                  pl.BlockSpec((1, TOK_BLK, d), lambda b, i: (b, i + n_ctx_blk, 0)),
                  pl.BlockSpec((1, 1, 6, d), lambda b, i: (b, 1, 0, 0)),
                  pl.BlockSpec((1, d), lambda b, i: (0, 0)),
                  pl.BlockSpec((1, d), lambda b, i: (0, 0))],
        out_specs=pl.BlockSpec((1, TOK_BLK, d), lambda b, i: (b, i, 0)),
        out_shape=jax.ShapeDtypeStruct((bsz, nb * TOK_BLK, d), F32),
        compiler_params=_cparams(("parallel", "arbitrary")),
        name="final",
    )(x1, f, mods, ln_g.reshape(1, d), ln_b.reshape(1, d))


def _token_tile(t, target):
    best = TOK_BLK
    for m in range(1, t // TOK_BLK + 1):
        cand = m * TOK_BLK
        if cand <= target and t % cand == 0:
            best = cand
    return best


def _ff_tile(ff, target):
    best = LANES
    for m in range(1, ff // LANES + 1):
        cand = m * LANES
        if cand <= target and ff % cand == 0:
            best = cand
    return best


def kernel(x, c, ctx, c_ctx, w_ada, b_ada, w_in, sgu_ln_g, sgu_ln_b, sgu_w, sgu_b, conv_w, a_log, dt_bias, gdn_norm_w, w_pa, w_pb, w_o, ln1_g, ln1_b, ln2_g, ln2_b, ffn_w1, ffn_w3, ffn_w2, moe_router, moe_w1, moe_w3, moe_w2):
    bsz, seq, d = x.shape
    ctx_len = ctx.shape[1]
    depth = w_ada.shape[0]
    alpha = (2.0 * depth) ** 0.25
    s = ctx_len + seq
    assert ctx_len % TOK_BLK == 0 and seq % TOK_BLK == 0
    assert d // 2 == SGU_GROUPS * LANES and bsz < 8
    n_ctx_blk = ctx_len // TOK_BLK
    sw = d // 2
    gw = GDN_WIDTH
    nh = GDN_HEADS

    cond = jnp.zeros((8, d), F32).at[:bsz].set(c).at[bsz].set(c_ctx)

    t_all = bsz * s
    tm_ffn = _token_tile(t_all, FFN_TOKEN_TILE)

    f_prev = None
    prev = None
    x_cur = x
    for l in range(depth):
        mod = _ada(cond, w_ada[l], b_ada[l]).reshape(8, 6, d)
        mods = jnp.stack([jnp.broadcast_to(mod[bsz], (bsz, 6, d)), mod[:bsz]], axis=1)

        g0 = 2 * sw + 4 * gw
        w_main = jnp.concatenate([w_in[l][:, :g0], w_in[l][:, g0 + 4 * nh:]], axis=1).astype(BF16)
        w_gate = jnp.pad(w_in[l][:, g0:g0 + 4 * nh], ((0, 0), (0, LANES - 4 * nh))).astype(BF16)
        outs = _inproj(x_cur, w_main, w_gate, mods, n_ctx_blk, prev=prev,
                       ctx=ctx if prev is None else None, alpha=alpha)
        x_cur, ug, vg, qkv, zs, sga, sgb, gates = outs

        gates_t = jnp.swapaxes(gates[:, :, :4 * nh], 1, 2)
        al = jnp.zeros((LANES,), F32)
        db = jnp.zeros((LANES,), F32)
        for dd in range(2):
            al = al.at[2 * nh * dd:2 * nh * dd + nh].set(a_log[l, dd])
            db = db.at[2 * nh * dd:2 * nh * dd + nh].set(dt_bias[l, dd])
        gp = jnp.zeros((8, LANES), F32).at[0].set(al).at[1].set(db)
        gpt = jnp.stack([jnp.broadcast_to(al[:16, None], (16, LANES)),
                         jnp.broadcast_to(db[:16, None], (16, LANES))])
        cw = jnp.pad(conv_w[l], ((0, 8 - CONV_K), (0, 0)))
        o_f, qkvn = _gdn(qkv, gates, gates_t, cw, gp, gpt, n_ctx_blk, reverse=False)
        o_b = _gdn(qkvn, gates, gates_t, None, gp, gpt, n_ctx_blk, reverse=True)

        sgu_bias = jnp.repeat(sgu_b[l].T, LANES, axis=1)
        is_moe = l % 2 == 1
        router = None
        if is_moe:
            r = moe_router[l // 2]
            router = (jnp.pad(r, ((0, 0), (0, LANES - r.shape[1]))), r.shape[1])
        outs = _merge(x_cur, ug, vg, o_f, o_b, zs, sga, sgb, mods, sgu_ln_g[l], sgu_ln_b[l],
                      sgu_w[l].astype(BF16), sgu_bias, gdn_norm_w[l], w_pa[l].astype(BF16),
                      w_pb[l].astype(BF16), w_o[l].astype(BF16), ln1_g[l], ln1_b[l], n_ctx_blk, alpha,
                      router=router)
        x1, h2 = outs[0], outs[1]

        h2f = h2.reshape(t_all, d)
        if is_moe:
            ne = moe_router.shape[-1]
            rcol = outs[2].reshape(t_all, LANES)
            rrow = outs[3].reshape(t_all // TOK_BLK, 16, TOK_BLK)
            tm = _token_tile(t_all, MOE_TOKEN_TILE)
            n_sub = tm // TOK_BLK
            cnt = outs[4][:, :, 0, :ne].astype(jnp.int32).reshape(t_all // tm, n_sub, ne)
            off = jnp.cumsum(cnt, axis=1) - cnt
            tot = jnp.sum(cnt, axis=1)
            tf = _ff_tile(moe_w1.shape[-1], 512)
            f = _moe(h2f, rcol, rrow, cnt.reshape(-1), off.reshape(-1), tot.reshape(-1),
                     moe_w1[l // 2].astype(BF16), moe_w3[l // 2].astype(BF16),
                     moe_w2[l // 2].astype(BF16), tm, tf)
        else:
            tf = _ff_tile(ffn_w1.shape[-1], 512)
            f = _ffn(h2f, ffn_w1[l // 2].astype(BF16), ffn_w3[l // 2].astype(BF16),
                     ffn_w2[l // 2].astype(BF16), tm_ffn, tf)
        f = f.reshape(bsz, s, d)
        prev = (f, mods, ln2_g[l], ln2_b[l])
        f_prev, mods_prev = f, mods
        x_cur = x1

    return _final(x_cur, f_prev, mods_prev, ln2_g[depth - 1], ln2_b[depth - 1], n_ctx_blk, alpha)
```

```python
import functools
import math

import jax
import jax.numpy as jnp
from jax import lax
from jax.experimental import pallas as pl
from jax.experimental.pallas import tpu as pltpu

F32 = jnp.float32
BF16 = jnp.bfloat16

LN_EPS = 1e-6
SGU_CHUNK = 128
SGU_GROUPS = 4
GDN_HEADS = 4
GDN_HEAD_DIM = 128
GDN_WIDTH = GDN_HEADS * GDN_HEAD_DIM
GDN_CHUNK = 64
CONV_K = 5
TOP_K = 2

LANES = 128
TOK_BLK = 256
HALO_ROWS = 16
GDN_BATCH_GROUP = 2
VMEM_LIMIT = 56 * 1024 * 1024


def _cparams(sem):
    return pltpu.CompilerParams(dimension_semantics=sem, vmem_limit_bytes=VMEM_LIMIT)


def _dot(a, b):
    return jnp.dot(a, b, preferred_element_type=F32)


def _dot_nt(a, b):
    return lax.dot_general(a, b, (((1,), (1,)), ((), ())), preferred_element_type=F32)


def _dot_tn(a, b):
    return lax.dot_general(a, b, (((0,), (0,)), ((), ())), preferred_element_type=F32)


def _split(a):
    hi = a.astype(BF16)
    lo = (a - hi.astype(F32)).astype(BF16)
    return hi, lo


def _dot3(a, b):
    ah, al = _split(a)
    bh, bl = _split(b)
    return _dot(ah, bh) + (_dot(ah, bl) + _dot(al, bh))


def _dot2_exact_lhs(a_bf, b):
    bh, bl = _split(b)
    return _dot(a_bf, bh) + _dot(a_bf, bl)


def _dot2_exact_rhs(a, b_bf):
    ah, al = _split(a)
    return _dot(ah, b_bf) + _dot(al, b_bf)


def _norm(x):
    mu = jnp.mean(x, axis=-1, keepdims=True)
    xc = x - mu
    var = jnp.mean(xc * xc, axis=-1, keepdims=True)
    return xc * lax.rsqrt(var + LN_EPS)


def _sigmoid(x):
    return 1.0 / (1.0 + jnp.exp(-x))


def _silu(x):
    return x * _sigmoid(x)


def _gelu(x):
    return 0.5 * x * (1.0 + lax.erf(x * (2.0 ** -0.5)))


def _blk_id(idx, size):
    return jnp.right_shift(idx, int(math.log2(size)))


def _softplus(x):
    return jnp.maximum(x, 0.0) + jnp.log1p(jnp.exp(-jnp.abs(x)))


def _ada_kernel(c_ref, w_ref, b_ref, o_ref):
    o_ref[...] = _dot3(_silu(c_ref[...]), w_ref[...]) + b_ref[...]


def _ada(cond, w, b):
    rows, d = cond.shape
    n = w.shape[1]
    tn = n // 6
    return pl.pallas_call(
        _ada_kernel,
        grid=(n // tn,),
        in_specs=[pl.BlockSpec((rows, d), lambda j: (0, 0)),
                  pl.BlockSpec((d, tn), lambda j: (0, j)),
                  pl.BlockSpec((1, tn), lambda j: (0, j))],
        out_specs=pl.BlockSpec((rows, tn), lambda j: (0, j)),
        out_shape=jax.ShapeDtypeStruct((rows, n), F32),
        compiler_params=_cparams(("arbitrary",)),
        name="ada",
    )(cond, w, b.reshape(1, n))


def _inproj_kernel(*refs, has_prev, alpha, seg, n_ctx_blk):
    if has_prev:
        (x_ref, f_ref, mprev_ref, lng_ref, lnb_ref, mods_ref, w_ref, wg_ref,
         xo_ref, ug_ref, vg_ref, qkv_ref, zs_ref, sga_ref, sgb_ref, gates_ref) = refs
        x = alpha * x_ref[0] + mprev_ref[0, 0, 5:6, :] * f_ref[0].astype(F32)
        x = _norm(x) * lng_ref[...] + lnb_ref[...]
        xo_ref[0] = x
    else:
        (c_ref, x_ref, mods_ref, w_ref, wg_ref,
         xo_ref, ug_ref, vg_ref, qkv_ref, zs_ref, sga_ref, sgb_ref, gates_ref) = refs
        x = jnp.where(pl.program_id(1) < n_ctx_blk, c_ref[0], x_ref[0])
        xo_ref[0] = x
    h = _norm(x) * (1.0 + mods_ref[0, 0, 1:2, :]) + mods_ref[0, 0, 0:1, :]
    hb = h.astype(BF16)

    def proj(lo, hi):
        return _dot(hb, w_ref[:, lo:hi])

    sw, gw = seg
    c0 = 0
    ug_ref[0] = _gelu(proj(c0, c0 + sw)).astype(BF16)
    c0 += sw
    vg_ref[0] = _gelu(proj(c0, c0 + sw)).astype(BF16)
    c0 += sw
    for j in range(3):
        qkv_ref[0, :, j * gw:(j + 1) * gw] = proj(c0, c0 + gw).astype(BF16)
        c0 += gw
    zs_ref[0] = _silu(proj(c0, c0 + gw)).astype(BF16)
    c0 += gw
    d = sga_ref.shape[-1]
    half = d // 2
    for j in range(2):
        sga_ref[0, :, j * half:(j + 1) * half] = _sigmoid(proj(c0, c0 + half)).astype(BF16)
        c0 += half
    for j in range(2):
        sgb_ref[0, :, j * half:(j + 1) * half] = _sigmoid(proj(c0, c0 + half)).astype(BF16)
        c0 += half
    gates_ref[0] = _dot(hb, wg_ref[...])


def _inproj(x, w_main, w_gate, mods, n_ctx_blk, prev=None, ctx=None, alpha=1.0):
    bsz, s, d = x.shape
    if ctx is not None:
        s += ctx.shape[1]
    nb = s // TOK_BLK
    sw = d // 2
    gw = GDN_WIDTH
    n_main = w_main.shape[1]

    def tok(width):
        return pl.BlockSpec((1, TOK_BLK, width), lambda b, i: (b, i, 0))

    def mod_spec():
        return pl.BlockSpec((1, 1, 6, d), lambda b, i: (b, jnp.where(i < n_ctx_blk, 0, 1), 0, 0))

    def full(shape):
        return pl.BlockSpec(shape, lambda b, i: (0,) * len(shape))

    if ctx is not None:
        in_specs = [pl.BlockSpec((1, TOK_BLK, d), lambda b, i: (b, jnp.minimum(i, n_ctx_blk - 1), 0)),
                    pl.BlockSpec((1, TOK_BLK, d), lambda b, i: (b, jnp.maximum(i - n_ctx_blk, 0), 0))]
        args = [ctx, x]
    else:
        in_specs = [tok(d)]
        args = [x]
    if prev is not None:
        f_prev, mods_prev, ln_g, ln_b = prev
        in_specs += [tok(d), mod_spec(), full((1, d)), full((1, d))]
        args += [f_prev, mods_prev, ln_g.reshape(1, d), ln_b.reshape(1, d)]
    in_specs += [mod_spec(), full((d, n_main)), full((d, LANES))]
    args += [mods, w_main, w_gate]

    out_specs = []
    out_shape = []
    out_specs.append(tok(d))
    out_shape.append(jax.ShapeDtypeStruct((bsz, s, d), F32))
    for width, dt in ((sw, BF16), (sw, BF16), (3 * gw, BF16), (gw, BF16), (d, BF16), (d, BF16),
                      (LANES, F32)):
        out_specs.append(tok(width))
        out_shape.append(jax.ShapeDtypeStruct((bsz, s, width), dt))

    return pl.pallas_call(
        functools.partial(_inproj_kernel, has_prev=prev is not None, alpha=alpha, seg=(sw, gw),
                          n_ctx_blk=n_ctx_blk),
        grid=(bsz, nb),
        in_specs=in_specs,
        out_specs=out_specs,
        out_shape=out_shape,
        compiler_params=_cparams(("parallel", "arbitrary")),
        name="inproj",
    )(*args)


def _gdn_block(i, n_blk, n_ctx_blk, reverse):
    if not reverse:
        return i
    return jnp.where(i < n_ctx_blk, n_ctx_blk - 1 - i, n_blk - 1 - (i - n_ctx_blk))


def _gdn_kernel(*refs, reverse, n_blk, n_ctx_blk):
    if reverse:
        qkvn_ref, g_ref, gt_ref, gp_ref, gpt_ref, o_ref, s_ref = refs
    else:
        (qkv_ref, prev_ref, next_ref, g_ref, gt_ref, cw_ref, gp_ref, gpt_ref,
         o_ref, qkvn_ref, win_ref, s_ref) = refs
    i = pl.program_id(1)
    blk = _gdn_block(i, n_blk, n_ctx_blk, reverse)
    tb = TOK_BLK
    hd = GDN_HEAD_DIM
    nh = GDN_HEADS
    d_idx = 1 if reverse else 0
    group = range(o_ref.shape[0])

    @pl.when(i == 0)
    def _():
        s_ref[...] = jnp.zeros_like(s_ref)

    if not reverse:
        first = jnp.logical_or(blk == 0, blk == n_ctx_blk)
        last = jnp.logical_or(blk == n_ctx_blk - 1, blk == n_blk - 1)
        for bb in group:
            prev = prev_ref[bb].astype(F32)[HALO_ROWS - 8:, :]
            nxt = next_ref[bb].astype(F32)[:8, :]
            win_ref[bb, 0:8, :] = jnp.where(first, 0.0, prev)
            win_ref[bb, 8:8 + tb, :] = qkv_ref[bb].astype(F32)
            win_ref[bb, 8 + tb:16 + tb, :] = jnp.where(last, 0.0, nxt)

    def conv_slab(bb, col):
        acc = None
        for j in range(CONV_K):
            off = 8 - CONV_K // 2 + j
            term = cw_ref[j:j + 1, col:col + hd] * win_ref[bb, off:off + tb, col:col + hd]
            acc = term if acc is None else acc + term
        return _silu(acc)

    def l2n(x):
        return x * lax.rsqrt(jnp.sum(x * x, axis=-1, keepdims=True) + LN_EPS)

    def qkv_slab(bb, col, kind):
        if reverse:
            return qkvn_ref[bb, :, col:col + hd].astype(F32)
        x = conv_slab(bb, col)
        if kind == 0:
            x = l2n(x) * (hd ** -0.5)
        elif kind == 1:
            x = l2n(x)
        qkvn_ref[bb, :, col:col + hd] = x.astype(BF16)
        return x

    lg_c = [-jnp.exp(gp_ref[0:1, :]) * _softplus(g_ref[bb] + gp_ref[1:2, :]) for bb in group]
    beta_c = [_sigmoid(g_ref[bb]) for bb in group]
    lg_r = [-jnp.exp(gpt_ref[0][:, 0:1]) * _softplus(gt_ref[bb] + gpt_ref[1][:, 0:1])
            for bb in group]

    ii = lax.broadcasted_iota(jnp.int32, (tb, tb), 0)
    jj = lax.broadcasted_iota(jnp.int32, (tb, tb), 1)
    same_chunk = _blk_id(ii, GDN_CHUNK) == _blk_id(jj, GDN_CHUNK)
    cum = jnp.logical_and(same_chunk, (ii <= jj) if reverse else (ii >= jj))
    cum_c = jnp.where(cum, 1.0, 0.0).astype(BF16)
    cum_r = jnp.where(jnp.logical_and(same_chunk, (jj <= ii) if reverse else (jj >= ii)),
                      1.0, 0.0).astype(BF16)
    ones_c = jnp.where(same_chunk, 1.0, 0.0).astype(BF16)
    gam_c = [_dot2_exact_lhs(cum_c, lg_c[bb]) for bb in group]
    tot_c = [_dot2_exact_lhs(ones_c, lg_c[bb]) for bb in group]
    gam_r = [_dot2_exact_rhs(lg_r[bb], cum_r) for bb in group]

    sc_rows = 2 * GDN_CHUNK
    pi = lax.broadcasted_iota(jnp.int32, (sc_rows, sc_rows), 0)
    pj = lax.broadcasted_iota(jnp.int32, (sc_rows, sc_rows), 1)
    tri_incl = (pi <= pj) if reverse else (pi >= pj)
    tri_strict = (pi < pj) if reverse else (pi > pj)

    def same(level):
        return _blk_id(pi, level) == _blk_id(pj, level)

    incl = jnp.logical_and(same(GDN_CHUNK), tri_incl)
    strict = jnp.logical_and(same(GDN_CHUNK), tri_strict)
    m16 = jnp.logical_and(strict, same(16))
    m32 = jnp.logical_and(jnp.logical_and(strict, same(32)), jnp.logical_not(same(16)))
    m64 = jnp.logical_and(strict, jnp.logical_not(same(32)))
    eye = jnp.where(pi == pj, 1.0, 0.0)

    n_sc = tb // sc_rows
    sc_order = range(n_sc - 1, -1, -1) if reverse else range(n_sc)
    ch_order = (1, 0) if reverse else (0, 1)

    heads = [(bb, h) for bb in group for h in range(nh)]
    tiles = [(u, sc) for u in heads for sc in range(n_sc)]
    q_all = {u: qkv_slab(u[0], u[1] * hd, 0) for u in heads}
    k_all = {u: qkv_slab(u[0], GDN_WIDTH + u[1] * hd, 1) for u in heads}
    v_all = {u: qkv_slab(u[0], 2 * GDN_WIDTH + u[1] * hd, 2) for u in heads}

    def rows(a, sc):
        return a[sc * sc_rows:(sc + 1) * sc_rows]

    def col_a(u):
        return 2 * nh * d_idx + u[1]

    q_t = {t: rows(q_all[t[0]], t[1]) for t in tiles}
    k_t = {t: rows(k_all[t[0]], t[1]) for t in tiles}
    v_t = {t: rows(v_all[t[0]], t[1]) for t in tiles}
    gcol = {(u, sc): rows(gam_c[u[0]], sc)[:, col_a(u):col_a(u) + 1] for u, sc in tiles}
    tcol = {(u, sc): rows(tot_c[u[0]], sc)[:, col_a(u):col_a(u) + 1] for u, sc in tiles}
    bcol = {(u, sc): rows(beta_c[u[0]], sc)[:, col_a(u) + nh:col_a(u) + nh + 1] for u, sc in tiles}
    grow = {(u, sc): gam_r[u[0]][col_a(u):col_a(u) + 1, sc * sc_rows:(sc + 1) * sc_rows]
            for u, sc in tiles}
    decay = {t: jnp.exp(jnp.where(incl, gcol[t] - grow[t], -jnp.inf)) for t in tiles}
    eg = {t: jnp.exp(gcol[t]) for t in tiles}
    kb = {t: k_t[t].astype(BF16) for t in tiles}
    qb = {t: q_t[t].astype(BF16) for t in tiles}
    kk = {t: _dot_nt(kb[t], kb[t]) for t in tiles}
    qk = {t: (_dot_nt(qb[t], kb[t]) * decay[t]).astype(BF16) for t in tiles}
    neg_l = {t: jnp.where(strict, -(bcol[t] * kk[t] * decay[t]), 0.0) for t in tiles}
    x1 = {t: jnp.where(m16, neg_l[t], 0.0).astype(BF16) for t in tiles}
    p = {t: eye + x1[t].astype(F32) for t in tiles}
    xp = x1
    for _ in range(3):
        xp = {t: _dot(xp[t], xp[t]).astype(BF16) for t in tiles}
        p = {t: p[t] + _dot(p[t].astype(BF16), xp[t]) for t in tiles}
    for mask in (m32, m64):
        pb = {t: p[t].astype(BF16) for t in tiles}
        pe = {t: _dot(pb[t], jnp.where(mask, neg_l[t], 0.0).astype(BF16)).astype(BF16) for t in tiles}
        p = {t: p[t] + _dot(pe[t], pb[t]) for t in tiles}
    rhs = {t: jnp.concatenate([bcol[t] * v_t[t], (bcol[t] * eg[t]) * k_t[t]], axis=1).astype(BF16)
           for t in tiles}
    sol = {t: _dot(p[t].astype(BF16), rhs[t]) for t in tiles}
    kd = {t: (k_t[t] * jnp.exp(tcol[t] - gcol[t])).astype(BF16) for t in tiles}
    wkqg = {t: jnp.concatenate([sol[t][:, hd:], q_t[t] * eg[t]], axis=0).astype(BF16) for t in tiles}

    s = {u: s_ref[u[0], u[1]] for u in heads}
    for sc in sc_order:
        w_parts = {u: [None, None] for u in heads}
        og_parts = {u: [None, None] for u in heads}
        for ch in ch_order:
            c0 = ch * GDN_CHUNK
            for h in heads:
                t = (h, sc)
                lhs = jnp.concatenate([wkqg[t][c0:c0 + GDN_CHUNK],
                                       wkqg[t][sc_rows + c0:sc_rows + c0 + GDN_CHUNK]], axis=0)
                prod = _dot(lhs, s[h].astype(BF16))
                w_ch = sol[t][c0:c0 + GDN_CHUNK, :hd] - prod[:GDN_CHUNK]
                og_parts[h][ch] = prod[GDN_CHUNK:]
                w_parts[h][ch] = w_ch
                gc = jnp.exp(tcol[t][c0:c0 + 1, :])
                s[h] = gc * s[h] + _dot_tn(kd[t][c0:c0 + GDN_CHUNK], w_ch.astype(BF16))
        for h in heads:
            w_full = jnp.concatenate(w_parts[h], axis=0).astype(BF16)
            o = jnp.concatenate(og_parts[h], axis=0) + _dot(qk[(h, sc)], w_full)
            o_ref[h[0], sc * sc_rows:(sc + 1) * sc_rows, h[1] * hd:(h[1] + 1) * hd] = o.astype(o_ref.dtype)
    for h in heads:
        s_ref[h[0], h[1]] = s[h]


def _gdn(qkv, gates, gates_t, conv_w, gp, gpt, n_ctx_blk, reverse):
    bsz, s, w3 = qkv.shape
    nb = s // TOK_BLK
    halo_per_blk = TOK_BLK // HALO_ROWS
    n_halo = s // HALO_ROWS

    def blk(i):
        return _gdn_block(i, nb, n_ctx_blk, reverse)

    grp = GDN_BATCH_GROUP if bsz % GDN_BATCH_GROUP == 0 else 1
    tok_qkv = pl.BlockSpec((grp, TOK_BLK, w3), lambda b, i: (b, blk(i), 0))
    tok_o = pl.BlockSpec((grp, TOK_BLK, GDN_WIDTH), lambda b, i: (b, blk(i), 0))
    gate_specs = [pl.BlockSpec((grp, TOK_BLK, LANES), lambda b, i: (b, blk(i), 0)),
                  pl.BlockSpec((grp, 16, TOK_BLK), lambda b, i: (b, 0, blk(i)))]
    par_specs = [pl.BlockSpec(gp.shape, lambda b, i: (0, 0)),
                 pl.BlockSpec(gpt.shape, lambda b, i: (0, 0, 0))]
    state = pltpu.VMEM((grp, GDN_HEADS, GDN_HEAD_DIM, GDN_HEAD_DIM), F32)
    o_shape = jax.ShapeDtypeStruct((bsz, s, GDN_WIDTH), BF16)
    body = functools.partial(_gdn_kernel, reverse=reverse, n_blk=nb, n_ctx_blk=n_ctx_blk)
    if reverse:
        return pl.pallas_call(
            body, grid=(bsz // grp, nb),
            in_specs=[tok_qkv] + gate_specs + par_specs,
            out_specs=tok_o, out_shape=o_shape, scratch_shapes=[state],
            compiler_params=_cparams(("parallel", "arbitrary")), name="gdn_bwd",
        )(qkv, gates, gates_t, gp, gpt)
    halo_specs = [
        pl.BlockSpec((grp, HALO_ROWS, w3),
                     lambda b, i: (b, jnp.maximum(blk(i) * halo_per_blk - 1, 0), 0)),
        pl.BlockSpec((grp, HALO_ROWS, w3),
                     lambda b, i: (b, jnp.minimum((blk(i) + 1) * halo_per_blk, n_halo - 1), 0))]
    return pl.pallas_call(
        body, grid=(bsz // grp, nb),
        in_specs=[tok_qkv] + halo_specs + gate_specs + [pl.BlockSpec(conv_w.shape, lambda b, i: (0, 0))]
        + par_specs,
        out_specs=[tok_o, tok_qkv],
        out_shape=[o_shape, jax.ShapeDtypeStruct((bsz, s, w3), BF16)],
        scratch_shapes=[pltpu.VMEM((grp, TOK_BLK + 16, w3), F32), state],
        compiler_params=_cparams(("parallel", "arbitrary")), name="gdn_fwd",
    )(qkv, qkv, qkv, gates, gates_t, conv_w, gp, gpt)


def _merge_kernel(*refs, alpha, n_experts):
    (x_ref, ug_ref, vg_ref, of_ref, ob_ref, zs_ref, sga_ref, sgb_ref, mods_ref,
     slg_ref, slb_ref, sw_ref, sbias_ref, nw_ref, wpa_ref, wpb_ref, wo_ref,
     l1g_ref, l1b_ref) = refs[:19]
    if n_experts:
        router_ref, x1_ref, h2_ref, rcol_ref, rrow_ref, cnt_ref = refs[19:]
    else:
        x1_ref, h2_ref = refs[19:]
    tb = TOK_BLK
    vn = (_norm(vg_ref[0].astype(F32)) * slg_ref[...] + slb_ref[...]).astype(BF16)
    gd = vn.shape[1] // SGU_GROUPS
    rows = []
    for c in range(tb // SGU_CHUNK):
        cols = []
        for g in range(SGU_GROUPS):
            cols.append(_dot(sw_ref[g], vn[c * SGU_CHUNK:(c + 1) * SGU_CHUNK, g * gd:(g + 1) * gd]))
        rows.append(jnp.concatenate(cols, axis=1) + sbias_ref[...])
    mixed = jnp.concatenate(rows, axis=0)
    oa = (ug_ref[0].astype(F32) * mixed).astype(BF16)
    osum = of_ref[0].astype(F32) + ob_ref[0].astype(F32)
    heads = []
    for h in range(GDN_HEADS):
        oh = osum[:, h * GDN_HEAD_DIM:(h + 1) * GDN_HEAD_DIM]
        heads.append(oh * lax.rsqrt(jnp.mean(oh * oh, axis=-1, keepdims=True) + LN_EPS) * nw_ref[...])
    ogd = (jnp.concatenate(heads, axis=1) * zs_ref[0].astype(F32)).astype(BF16)
    y = (sga_ref[0].astype(F32) * _dot(oa, wpa_ref[...])
         + sgb_ref[0].astype(F32) * _dot(ogd, wpb_ref[...]))
    t = _dot(y.astype(BF16), wo_ref[...])
    x1 = _norm(alpha * x_ref[0] + mods_ref[0, 0, 2:3, :] * t) * l1g_ref[...] + l1b_ref[...]
    x1_ref[0] = x1
    h2 = _norm(x1) * (1.0 + mods_ref[0, 0, 4:5, :]) + mods_ref[0, 0, 3:4, :]
    h2_ref[0] = h2.astype(BF16)
    if n_experts:
        logits = _dot3(h2, router_ref[...])
        lane = lax.broadcasted_iota(jnp.int32, logits.shape, 1).astype(F32)
        lg = jnp.where(lane < n_experts, logits, -jnp.inf)
        m1 = jnp.max(lg, axis=-1, keepdims=True)
        i1 = jnp.min(jnp.where(lg == m1, lane, float(LANES)), axis=-1, keepdims=True)
        lg2 = jnp.where(lane == i1, -jnp.inf, lg)
        m2 = jnp.max(lg2, axis=-1, keepdims=True)
        i2 = jnp.min(jnp.where(lg2 == m2, lane, float(LANES)), axis=-1, keepdims=True)
        e2 = jnp.exp(m2 - m1)
        den = 1.0 + e2
        gate = jnp.where(lane == i1, 1.0 / den, 0.0) + jnp.where(lane == i2, e2 / den, 0.0)
        sel = jnp.where(jnp.logical_or(lane == i1, lane == i2), 1.0, 0.0)
        selb = sel.astype(BF16)
        ti = lax.broadcasted_iota(jnp.int32, (tb, tb), 0)
        tj = lax.broadcasted_iota(jnp.int32, (tb, tb), 1)
        before = jnp.where(tj < ti, 1.0, 0.0).astype(BF16)
        rcol_ref[0] = jnp.where(sel > 0.0, _dot(before, selb), MOE_NOT_SELECTED)
        er = lax.broadcasted_iota(jnp.int32, (8, LANES), 0)
        ec = lax.broadcasted_iota(jnp.int32, (8, LANES), 1)
        eye8 = jnp.where(er == ec, 1.0, 0.0).astype(BF16)
        sel_t = _dot_nt(eye8, selb)
        after = jnp.where(ti < tj, 1.0, 0.0).astype(BF16)
        rrow_ref[0, 0, 0:8, :] = jnp.where(sel_t > 0.0, _dot(sel_t.astype(BF16), after), MOE_NOT_SELECTED)
        g_hi = gate.astype(BF16)
        g_mid = (gate - g_hi.astype(F32)).astype(BF16)
        g_lo = (gate - g_hi.astype(F32) - g_mid.astype(F32)).astype(BF16)
        rrow_ref[0, 0, 8:16, :] = _dot_nt(eye8, g_hi) + (_dot_nt(eye8, g_mid) + _dot_nt(eye8, g_lo))
        cnt_ref[0, 0] = _dot(jnp.ones((8, tb), BF16), selb)


def _merge(x, ug, vg, o_f, o_b, zs, sga, sgb, mods, sgu_ln_g, sgu_ln_b, sgu_w, sgu_bias, norm_w,
           w_pa, w_pb, w_o, ln_g, ln_b, n_ctx_blk, alpha, router=None):
    bsz, s, d = x.shape
    nb = s // TOK_BLK
    sw = d // 2
    n_experts = 0 if router is None else router[1]

    def tok(width):
        return pl.BlockSpec((1, TOK_BLK, width), lambda b, i: (b, i, 0))

    def full(a):
        return pl.BlockSpec(a.shape, lambda b, i: (0,) * a.ndim)

    consts = [sgu_ln_g.reshape(1, sw), sgu_ln_b.reshape(1, sw), sgu_w, sgu_bias,
              norm_w.reshape(1, GDN_HEAD_DIM), w_pa, w_pb, w_o, ln_g.reshape(1, d), ln_b.reshape(1, d)]
    if n_experts:
        consts.append(router[0])
    in_specs = [tok(d), tok(sw), tok(sw), tok(GDN_WIDTH), tok(GDN_WIDTH), tok(GDN_WIDTH), tok(d), tok(d),
                pl.BlockSpec((1, 1, 6, d), lambda b, i: (b, jnp.where(i < n_ctx_blk, 0, 1), 0, 0))]
    in_specs += [full(a) for a in consts]
    out_specs = [tok(d), tok(d)]
    out_shape = [jax.ShapeDtypeStruct((bsz, s, d), F32), jax.ShapeDtypeStruct((bsz, s, d), BF16)]
    if n_experts:
        assert n_experts <= 8
        out_specs += [tok(LANES),
                      pl.BlockSpec((1, 1, 16, TOK_BLK), lambda b, i: (b, i, 0, 0)),
                      pl.BlockSpec((1, 1, 8, LANES), lambda b, i: (b, i, 0, 0))]
        out_shape += [jax.ShapeDtypeStruct((bsz, s, LANES), F32),
                      jax.ShapeDtypeStruct((bsz, nb, 16, TOK_BLK), F32),
                      jax.ShapeDtypeStruct((bsz, nb, 8, LANES), F32)]
    return pl.pallas_call(
        functools.partial(_merge_kernel, alpha=alpha, n_experts=n_experts),
        grid=(bsz, nb),
        in_specs=in_specs,
        out_specs=out_specs,
        out_shape=out_shape,
        compiler_params=_cparams(("parallel", "arbitrary")),
        name="merge",
    )(x, ug, vg, o_f, o_b, zs, sga, sgb, mods, *consts)


def _ffn_kernel(h_ref, w1_ref, w3_ref, w2_ref, o_ref, acc_ref):
    f = pl.program_id(1)

    @pl.when(f == 0)
    def _():
        acc_ref[...] = jnp.zeros_like(acc_ref)

    h = h_ref[...]
    mid = (_silu(_dot(h, w1_ref[...])) * _dot(h, w3_ref[...])).astype(BF16)
    acc_ref[...] += _dot(mid, w2_ref[...])

    @pl.when(f == pl.num_programs(1) - 1)
    def _():
        o_ref[...] = acc_ref[...].astype(o_ref.dtype)


def _ffn(h, w1, w3, w2, tm, tf):
    t, d = h.shape
    ff = w1.shape[1]
    return pl.pallas_call(
        _ffn_kernel,
        grid=(t // tm, ff // tf),
        in_specs=[pl.BlockSpec((tm, d), lambda i, f: (i, 0)),
                  pl.BlockSpec((d, tf), lambda i, f: (0, f)),
                  pl.BlockSpec((d, tf), lambda i, f: (0, f)),
                  pl.BlockSpec((tf, d), lambda i, f: (f, 0))],
        out_specs=pl.BlockSpec((tm, d), lambda i, f: (i, 0)),
        out_shape=jax.ShapeDtypeStruct((t, d), BF16),
        scratch_shapes=[pltpu.VMEM((tm, d), F32)],
        compiler_params=_cparams(("parallel", "arbitrary")),
        name="ffn",
    )(h, w1, w3, w2)


MOE_GATHER_ROWS = 128
MOE_FFN_ROWS = 512
MOE_ROW_ALIGN = 16
MOE_NOT_SELECTED = -4096.0
MOE_TOKEN_TILE = 3072
FFN_TOKEN_TILE = 1536


def _moe_rows_capacity(tm):
    rows = tm + TOK_BLK + LANES
    return -(-rows // MOE_FFN_ROWS) * MOE_FFN_ROWS


def _moe_kernel(cnt_ref, off_ref, tot_ref, h_ref, rcol_ref, rrow_ref, w1_ref, w3_ref, w2_ref,
                o_ref, xs_ref, ys_ref, wrow_ref, *, n_experts):
    i = pl.program_id(0)
    e = pl.program_id(1)
    f = pl.program_id(2)
    tm = h_ref.shape[0]
    n_sub = tm // TOK_BLK
    gr = MOE_GATHER_ROWS
    fr = MOE_FFN_ROWS
    tot = tot_ref[i * n_experts + e]
    quarter = fr // 4
    n_quarters = _blk_id(tot + (quarter - 1), quarter)
    n_full = _blk_id(n_quarters, 4)
    has_half = jnp.bitwise_and(n_quarters, 2) != 0
    has_quarter = jnp.bitwise_and(n_quarters, 1) != 0

    def piece(j):
        base = (i * n_sub + j) * n_experts + e
        off = off_ref[base]
        start = pl.multiple_of(_blk_id(off, MOE_ROW_ALIGN) * MOE_ROW_ALIGN, MOE_ROW_ALIGN)
        return start, off - start, cnt_ref[base]

    @pl.when(jnp.logical_and(e == 0, f == 0))
    def _():
        o_ref[...] = jnp.zeros_like(o_ref)
        xs_ref[...] = jnp.zeros_like(xs_ref)
        ys_ref[...] = jnp.zeros_like(ys_ref)
        wrow_ref[...] = jnp.zeros_like(wrow_ref)

    @pl.when(f == 0)
    def _():
        def clear(rb, carry):
            rows = pl.ds(pl.multiple_of(rb * fr, fr), fr)
            xs_ref[rows, :] = jnp.zeros((fr, xs_ref.shape[1]), BF16)
            ys_ref[rows, :] = jnp.zeros((fr, ys_ref.shape[1]), F32)
            wrow_ref[rows, :] = jnp.zeros((fr, LANES), F32)
            return carry

        lax.fori_loop(0, _blk_id(tot + (fr - 1), fr) + 1, clear, 0)

        row_id = lax.broadcasted_iota(jnp.int32, (gr, TOK_BLK), 0).astype(F32)
        for j in range(n_sub):
            start, shift, cnt = piece(j)
            rank = rrow_ref[j, pl.ds(e, 1), :]
            gate = rrow_ref[j, pl.ds(8 + e, 1), :]
            row0 = row_id - shift.astype(F32)

            def gather(b, carry, j=j, start=start, rank=rank, gate=gate, row0=row0):
                hit = rank == row0 + (b * gr).astype(F32)
                rows = _dot(jnp.where(hit, 1.0, 0.0).astype(BF16), h_ref[j * TOK_BLK:(j + 1) * TOK_BLK, :])
                dst = pl.ds(pl.multiple_of(start + b * gr, MOE_ROW_ALIGN), gr)
                xs_ref[dst, :] = xs_ref[dst, :] + rows.astype(BF16)
                wgt = jnp.sum(jnp.where(hit, gate, 0.0), axis=-1, keepdims=True)
                wrow_ref[dst, :] = wrow_ref[dst, :] + jnp.broadcast_to(wgt, (gr, LANES))
                return carry

            lax.fori_loop(0, _blk_id(shift + cnt + (gr - 1), gr), gather, 0)

    def ffn_rows(r0, n):
        x = xs_ref[pl.ds(r0, n), :]
        mid = _silu(_dot(x, w1_ref[0])) * _dot(x, w3_ref[0]) * wrow_ref[pl.ds(r0, n), :][:, 0:1]
        ys_ref[pl.ds(r0, n), :] += _dot(mid.astype(BF16), w2_ref[0])

    def ffn(rb, carry):
        ffn_rows(pl.multiple_of(rb * fr, fr), fr)
        return carry

    lax.fori_loop(0, n_full, ffn, 0)

    @pl.when(has_half)
    def _():
        ffn_rows(pl.multiple_of(n_full * fr, fr), 2 * quarter)

    @pl.when(has_quarter)
    def _():
        r0 = n_full * fr + jnp.where(has_half, 2 * quarter, 0)
        ffn_rows(pl.multiple_of(r0, quarter), quarter)

    @pl.when(f == pl.num_programs(2) - 1)
    def _():
        col_id = lax.broadcasted_iota(jnp.int32, (TOK_BLK, LANES), 1).astype(F32)
        lane = lax.broadcasted_iota(jnp.int32, (TOK_BLK, LANES), 1)
        for j in range(n_sub):
            start, shift, cnt = piece(j)
            tok = slice(j * TOK_BLK, (j + 1) * TOK_BLK)

            def rank_col(tok=tok):
                return jnp.sum(jnp.where(lane == e, rcol_ref[tok, :], 0.0), axis=-1, keepdims=True)

            for w in range(TOK_BLK // LANES + 1):

                @pl.when(jnp.logical_and(cnt > 0, shift + cnt > w * LANES))
                def _(w=w, tok=tok, start=start, shift=shift, rank_col=rank_col):
                    cols = col_id - (shift - w * LANES).astype(F32)
                    onehot = jnp.where(rank_col() == cols, 1.0, 0.0).astype(BF16)
                    rows = pl.ds(pl.multiple_of(start + w * LANES, MOE_ROW_ALIGN), LANES)
                    o_ref[tok, :] += _dot(onehot, ys_ref[rows, :].astype(BF16))


def _moe(h, rcol, rrow, cnt, off, tot, w1, w3, w2, tm, tf):
    t, d = h.shape
    ne, _, ff = w1.shape
    n_sub = tm // TOK_BLK
    cap = _moe_rows_capacity(tm)
    once = pl.Buffered(1)
    grid_spec = pltpu.PrefetchScalarGridSpec(
        num_scalar_prefetch=3,
        grid=(t // tm, ne, ff // tf),
        in_specs=[pl.BlockSpec((tm, d), lambda i, e, f, *_: (i, 0), pipeline_mode=once),
                  pl.BlockSpec((tm, LANES), lambda i, e, f, *_: (i, 0), pipeline_mode=once),
                  pl.BlockSpec((n_sub, 16, TOK_BLK), lambda i, e, f, *_: (i, 0, 0)),
                  pl.BlockSpec((1, d, tf), lambda i, e, f, *_: (e, 0, f)),
                  pl.BlockSpec((1, d, tf), lambda i, e, f, *_: (e, 0, f)),
                  pl.BlockSpec((1, tf, d), lambda i, e, f, *_: (e, f, 0))],
        out_specs=pl.BlockSpec((tm, d), lambda i, e, f, *_: (i, 0), pipeline_mode=once),
        scratch_shapes=[pltpu.VMEM((cap, d), BF16), pltpu.VMEM((cap, d), F32),
                        pltpu.VMEM((cap, LANES), F32)],
    )
    return pl.pallas_call(
        functools.partial(_moe_kernel, n_experts=ne),
        grid_spec=grid_spec,
        out_shape=jax.ShapeDtypeStruct((t, d), F32),
        compiler_params=_cparams(("parallel", "arbitrary", "arbitrary")),
        name="moe",
    )(cnt, off, tot, h, rcol, rrow, w1, w3, w2)


def _final_kernel(x_ref, f_ref, mods_ref, g_ref, b_ref, o_ref, *, alpha):
    x = alpha * x_ref[0] + mods_ref[0, 0, 5:6, :] * f_ref[0].astype(F32)
    o_ref[0] = _norm(x) * g_ref[...] + b_ref[...]


def _final(x1, f, mods, ln_g, ln_b, n_ctx_blk, alpha):
    bsz, s, d = x1.shape
    nb = s // TOK_BLK - n_ctx_blk
    return pl.pallas_call(
        functools.partial(_final_kernel, alpha=alpha),
        grid=(bsz, nb),
        in_specs=[pl.BlockSpec((1, TOK_BLK, d), lambda b, i: (b, i + n_ctx_blk, 0)),
                  pl.BlockSpec((1, TOK_BLK, d), lambda b, i: (b, i + n_ctx_blk, 0)),
                  pl.BlockSpec((1, 1, 6, d), lambda b, i: (b, 1, 0, 0)),
                  pl.BlockSpec((1, d), lambda b, i: (0, 0)),
                  pl.BlockSpec((1, d), lambda b, i: (0, 0))],
        out_specs=pl.BlockSpec((1, TOK_BLK, d), lambda b, i: (b, i, 0)),
        out_shape=jax.ShapeDtypeStruct((bsz, nb * TOK_BLK, d), F32),
        compiler_params=_cparams(("parallel", "arbitrary")),
        name="final",
    )(x1, f, mods, ln_g.reshape(1, d), ln_b.reshape(1, d))


def _token_tile(t, target):
    best = TOK_BLK
    for m in range(1, t // TOK_BLK + 1):
        cand = m * TOK_BLK
        if cand <= target and t % cand == 0:
            best = cand
    return best


def _ff_tile(ff, target):
    best = LANES
    for m in range(1, ff // LANES + 1):
        cand = m * LANES
        if cand <= target and ff % cand == 0:
            best = cand
    return best


def kernel(x, c, ctx, c_ctx, w_ada, b_ada, w_in, sgu_ln_g, sgu_ln_b, sgu_w, sgu_b, conv_w, a_log, dt_bias, gdn_norm_w, w_pa, w_pb, w_o, ln1_g, ln1_b, ln2_g, ln2_b, ffn_w1, ffn_w3, ffn_w2, moe_router, moe_w1, moe_w3, moe_w2):
    bsz, seq, d = x.shape
    ctx_len = ctx.shape[1]
    depth = w_ada.shape[0]
    alpha = (2.0 * depth) ** 0.25
    s = ctx_len + seq
    assert ctx_len % TOK_BLK == 0 and seq % TOK_BLK == 0
    assert d // 2 == SGU_GROUPS * LANES and bsz < 8
    n_ctx_blk = ctx_len // TOK_BLK
    sw = d // 2
    gw = GDN_WIDTH
    nh = GDN_HEADS

    cond = jnp.zeros((8, d), F32).at[:bsz].set(c).at[bsz].set(c_ctx)

    t_all = bsz * s
    tm_ffn = _token_tile(t_all, FFN_TOKEN_TILE)

    f_prev = None
    prev = None
    x_cur = x
    for l in range(depth):
        mod = _ada(cond, w_ada[l], b_ada[l]).reshape(8, 6, d)
        mods = jnp.stack([jnp.broadcast_to(mod[bsz], (bsz, 6, d)), mod[:bsz]], axis=1)

        g0 = 2 * sw + 4 * gw
        w_main = jnp.concatenate([w_in[l][:, :g0], w_in[l][:, g0 + 4 * nh:]], axis=1).astype(BF16)
        w_gate = jnp.pad(w_in[l][:, g0:g0 + 4 * nh], ((0, 0), (0, LANES - 4 * nh))).astype(BF16)
        outs = _inproj(x_cur, w_main, w_gate, mods, n_ctx_blk, prev=prev,
                       ctx=ctx if prev is None else None, alpha=alpha)
        x_cur, ug, vg, qkv, zs, sga, sgb, gates = outs

        gates_t = jnp.swapaxes(gates[:, :, :4 * nh], 1, 2)
        al = jnp.zeros((LANES,), F32)
        db = jnp.zeros((LANES,), F32)
        for dd in range(2):
            al = al.at[2 * nh * dd:2 * nh * dd + nh].set(a_log[l, dd])
            db = db.at[2 * nh * dd:2 * nh * dd + nh].set(dt_bias[l, dd])
        gp = jnp.zeros((8, LANES), F32).at[0].set(al).at[1].set(db)
        gpt = jnp.stack([jnp.broadcast_to(al[:16, None], (16, LANES)),
                         jnp.broadcast_to(db[:16, None], (16, LANES))])
        cw = jnp.pad(conv_w[l], ((0, 8 - CONV_K), (0, 0)))
        o_f, qkvn = _gdn(qkv, gates, gates_t, cw, gp, gpt, n_ctx_blk, reverse=False)
        o_b = _gdn(qkvn, gates, gates_t, None, gp, gpt, n_ctx_blk, reverse=True)

        sgu_bias = jnp.repeat(sgu_b[l].T, LANES, axis=1)
        is_moe = l % 2 == 1
        router = None
        if is_moe:
            r = moe_router[l // 2]
            router = (jnp.pad(r, ((0, 0), (0, LANES - r.shape[1]))), r.shape[1])
        outs = _merge(x_cur, ug, vg, o_f, o_b, zs, sga, sgb, mods, sgu_ln_g[l], sgu_ln_b[l],
                      sgu_w[l].astype(BF16), sgu_bias, gdn_norm_w[l], w_pa[l].astype(BF16),
                      w_pb[l].astype(BF16), w_o[l].astype(BF16), ln1_g[l], ln1_b[l], n_ctx_blk, alpha,
                      router=router)
        x1, h2 = outs[0], outs[1]

        h2f = h2.reshape(t_all, d)
        if is_moe:
            ne = moe_router.shape[-1]
            rcol = outs[2].reshape(t_all, LANES)
            rrow = outs[3].reshape(t_all // TOK_BLK, 16, TOK_BLK)
            tm = _token_tile(t_all, MOE_TOKEN_TILE)
            n_sub = tm // TOK_BLK
            cnt = outs[4][:, :, 0, :ne].astype(jnp.int32).reshape(t_all // tm, n_sub, ne)
            off = jnp.cumsum(cnt, axis=1) - cnt
            tot = jnp.sum(cnt, axis=1)
            tf = _ff_tile(moe_w1.shape[-1], 512)
            f = _moe(h2f, rcol, rrow, cnt.reshape(-1), off.reshape(-1), tot.reshape(-1),
                     moe_w1[l // 2].astype(BF16), moe_w3[l // 2].astype(BF16),
                     moe_w2[l // 2].astype(BF16), tm, tf)
        else:
            tf = _ff_tile(ffn_w1.shape[-1], 512)
            f = _ffn(h2f, ffn_w1[l // 2].astype(BF16), ffn_w3[l // 2].astype(BF16),
                     ffn_w2[l // 2].astype(BF16), tm_ffn, tf)
        f = f.reshape(bsz, s, d)
        prev = (f, mods, ln2_g[l], ln2_b[l])
        f_prev, mods_prev = f, mods
        x_cur = x1

    return _final(x_cur, f_prev, mods_prev, ln2_g[depth - 1], ln2_b[depth - 1], n_ctx_blk, alpha)
```

```python
import functools
import math

import jax
import jax.numpy as jnp
from jax import lax
from jax.experimental import pallas as pl
from jax.experimental.pallas import tpu as pltpu

F32 = jnp.float32
BF16 = jnp.bfloat16

LN_EPS = 1e-6
SGU_CHUNK = 128
SGU_GROUPS = 4
GDN_HEADS = 4
GDN_HEAD_DIM = 128
GDN_WIDTH = GDN_HEADS * GDN_HEAD_DIM
GDN_CHUNK = 64
CONV_K = 5
TOP_K = 2

LANES = 128
TOK_BLK = 256
HALO_ROWS = 16
GDN_BATCH_GROUP = 4
VMEM_LIMIT = 56 * 1024 * 1024


def _cparams(sem):
    return pltpu.CompilerParams(dimension_semantics=sem, vmem_limit_bytes=VMEM_LIMIT)


def _dot(a, b):
    return jnp.dot(a, b, preferred_element_type=F32)


def _dot_nt(a, b):
    return lax.dot_general(a, b, (((1,), (1,)), ((), ())), preferred_element_type=F32)


def _dot_tn(a, b):
    return lax.dot_general(a, b, (((0,), (0,)), ((), ())), preferred_element_type=F32)


def _split(a):
    hi = a.astype(BF16)
    lo = (a - hi.astype(F32)).astype(BF16)
    return hi, lo


def _dot3(a, b):
    ah, al = _split(a)
    bh, bl = _split(b)
    return _dot(ah, bh) + (_dot(ah, bl) + _dot(al, bh))


def _dot2_exact_lhs(a_bf, b):
    bh, bl = _split(b)
    return _dot(a_bf, bh) + _dot(a_bf, bl)


def _dot2_exact_rhs(a, b_bf):
    ah, al = _split(a)
    return _dot(ah, b_bf) + _dot(al, b_bf)


def _norm(x):
    mu = jnp.mean(x, axis=-1, keepdims=True)
    xc = x - mu
    var = jnp.mean(xc * xc, axis=-1, keepdims=True)
    return xc * lax.rsqrt(var + LN_EPS)


def _sigmoid(x):
    return 1.0 / (1.0 + jnp.exp(-x))


def _silu(x):
    return x * _sigmoid(x)


def _gelu(x):
    return 0.5 * x * (1.0 + lax.erf(x * (2.0 ** -0.5)))


def _blk_id(idx, size):
    return jnp.right_shift(idx, int(math.log2(size)))


def _softplus(x):
    return jnp.maximum(x, 0.0) + jnp.log1p(jnp.exp(-jnp.abs(x)))


def _ada_kernel(c_ref, w_ref, b_ref, o_ref):
    o_ref[...] = _dot3(_silu(c_ref[...]), w_ref[...]) + b_ref[...]


def _ada(cond, w, b):
    rows, d = cond.shape
    n = w.shape[1]
    tn = n // 6
    return pl.pallas_call(
        _ada_kernel,
        grid=(n // tn,),
        in_specs=[pl.BlockSpec((rows, d), lambda j: (0, 0)),
                  pl.BlockSpec((d, tn), lambda j: (0, j)),
                  pl.BlockSpec((1, tn), lambda j: (0, j))],
        out_specs=pl.BlockSpec((rows, tn), lambda j: (0, j)),
        out_shape=jax.ShapeDtypeStruct((rows, n), F32),
        compiler_params=_cparams(("arbitrary",)),
        name="ada",
    )(cond, w, b.reshape(1, n))


def _inproj_kernel(*refs, has_prev, alpha, seg, n_ctx_blk):
    if has_prev:
        (x_ref, f_ref, mprev_ref, lng_ref, lnb_ref, mods_ref, w_ref, wg_ref,
         xo_ref, ug_ref, vg_ref, qkv_ref, zs_ref, sga_ref, sgb_ref, gates_ref) = refs
        x = alpha * x_ref[0] + mprev_ref[0, 0, 5:6, :] * f_ref[0].astype(F32)
        x = _norm(x) * lng_ref[...] + lnb_ref[...]
        xo_ref[0] = x
    else:
        (c_ref, x_ref, mods_ref, w_ref, wg_ref,
         xo_ref, ug_ref, vg_ref, qkv_ref, zs_ref, sga_ref, sgb_ref, gates_ref) = refs
        x = jnp.where(pl.program_id(1) < n_ctx_blk, c_ref[0], x_ref[0])
        xo_ref[0] = x
    h = _norm(x) * (1.0 + mods_ref[0, 0, 1:2, :]) + mods_ref[0, 0, 0:1, :]
    hb = h.astype(BF16)

    def proj(lo, hi):
        return _dot(hb, w_ref[:, lo:hi])

    sw, gw = seg
    c0 = 0
    ug_ref[0] = _gelu(proj(c0, c0 + sw)).astype(BF16)
    c0 += sw
    vg_ref[0] = _gelu(proj(c0, c0 + sw)).astype(BF16)
    c0 += sw
    for j in range(3):
        qkv_ref[0, :, j * gw:(j + 1) * gw] = proj(c0, c0 + gw).astype(BF16)
        c0 += gw
    zs_ref[0] = _silu(proj(c0, c0 + gw)).astype(BF16)
    c0 += gw
    d = sga_ref.shape[-1]
    half = d // 2
    for j in range(2):
        sga_ref[0, :, j * half:(j + 1) * half] = _sigmoid(proj(c0, c0 + half)).astype(BF16)
        c0 += half
    for j in range(2):
        sgb_ref[0, :, j * half:(j + 1) * half] = _sigmoid(proj(c0, c0 + half)).astype(BF16)
        c0 += half
    gates_ref[0] = _dot(hb, wg_ref[...])


def _inproj(x, w_main, w_gate, mods, n_ctx_blk, prev=None, ctx=None, alpha=1.0):
    bsz, s, d = x.shape
    if ctx is not None:
        s += ctx.shape[1]
    nb = s // TOK_BLK
    sw = d // 2
    gw = GDN_WIDTH
    n_main = w_main.shape[1]

    def tok(width):
        return pl.BlockSpec((1, TOK_BLK, width), lambda b, i: (b, i, 0))

    def mod_spec():
        return pl.BlockSpec((1, 1, 6, d), lambda b, i: (b, jnp.where(i < n_ctx_blk, 0, 1), 0, 0))

    def full(shape):
        return pl.BlockSpec(shape, lambda b, i: (0,) * len(shape))

    if ctx is not None:
        in_specs = [pl.BlockSpec((1, TOK_BLK, d), lambda b, i: (b, jnp.minimum(i, n_ctx_blk - 1), 0)),
                    pl.BlockSpec((1, TOK_BLK, d), lambda b, i: (b, jnp.maximum(i - n_ctx_blk, 0), 0))]
        args = [ctx, x]
    else:
        in_specs = [tok(d)]
        args = [x]
    if prev is not None:
        f_prev, mods_prev, ln_g, ln_b = prev
        in_specs += [tok(d), mod_spec(), full((1, d)), full((1, d))]
        args += [f_prev, mods_prev, ln_g.reshape(1, d), ln_b.reshape(1, d)]
    in_specs += [mod_spec(), full((d, n_main)), full((d, LANES))]
    args += [mods, w_main, w_gate]

    out_specs = []
    out_shape = []
    out_specs.append(tok(d))
    out_shape.append(jax.ShapeDtypeStruct((bsz, s, d), F32))
    for width, dt in ((sw, BF16), (sw, BF16), (3 * gw, BF16), (gw, BF16), (d, BF16), (d, BF16),
                      (LANES, F32)):
        out_specs.append(tok(width))
        out_shape.append(jax.ShapeDtypeStruct((bsz, s, width), dt))

    return pl.pallas_call(
        functools.partial(_inproj_kernel, has_prev=prev is not None, alpha=alpha, seg=(sw, gw),
                          n_ctx_blk=n_ctx_blk),
        grid=(bsz, nb),
        in_specs=in_specs,
        out_specs=out_specs,
        out_shape=out_shape,
        compiler_params=_cparams(("parallel", "arbitrary")),
        name="inproj",
    )(*args)


def _gdn_block(i, n_blk, n_ctx_blk, reverse):
    if not reverse:
        return i
    return jnp.where(i < n_ctx_blk, n_ctx_blk - 1 - i, n_blk - 1 - (i - n_ctx_blk))


def _gdn_kernel(*refs, reverse, n_blk, n_ctx_blk):
    if reverse:
        qkvn_ref, g_ref, gt_ref, gp_ref, gpt_ref, o_ref, s_ref = refs
    else:
        (qkv_ref, prev_ref, next_ref, g_ref, gt_ref, cw_ref, gp_ref, gpt_ref,
         o_ref, qkvn_ref, win_ref, s_ref) = refs
    i = pl.program_id(1)
    blk = _gdn_block(i, n_blk, n_ctx_blk, reverse)
    tb = TOK_BLK
    hd = GDN_HEAD_DIM
    nh = GDN_HEADS
    d_idx = 1 if reverse else 0
    group = range(o_ref.shape[0])

    @pl.when(i == 0)
    def _():
        s_ref[...] = jnp.zeros_like(s_ref)

    if not reverse:
        first = jnp.logical_or(blk == 0, blk == n_ctx_blk)
        last = jnp.logical_or(blk == n_ctx_blk - 1, blk == n_blk - 1)
        for bb in group:
            prev = prev_ref[bb].astype(F32)[HALO_ROWS - 8:, :]
            nxt = next_ref[bb].astype(F32)[:8, :]
            win_ref[bb, 0:8, :] = jnp.where(first, 0.0, prev)
            win_ref[bb, 8:8 + tb, :] = qkv_ref[bb].astype(F32)
            win_ref[bb, 8 + tb:16 + tb, :] = jnp.where(last, 0.0, nxt)

    def conv_slab(bb, col):
        win = win_ref[bb, :, col:col + hd]
        n = win.shape[0]
        acc = None
        for j in range(CONV_K):
            term = cw_ref[j:j + 1, col:col + hd] * win
            shift = (CONV_K // 2 - j) % n
            if shift:
                term = pltpu.roll(term, shift, axis=0)
            acc = term if acc is None else acc + term
        return _silu(acc[8:8 + tb])

    def l2n(x):
        return x * lax.rsqrt(jnp.sum(x * x, axis=-1, keepdims=True) + LN_EPS)

    def qkv_slab(bb, col, kind):
        if reverse:
            return qkvn_ref[bb, :, col:col + hd].astype(F32)
        x = conv_slab(bb, col)
        if kind == 0:
            x = l2n(x) * (hd ** -0.5)
        elif kind == 1:
            x = l2n(x)
        qkvn_ref[bb, :, col:col + hd] = x.astype(BF16)
        return x

    lg_c = [-jnp.exp(gp_ref[0:1, :]) * _softplus(g_ref[bb] + gp_ref[1:2, :]) for bb in group]
    beta_c = [_sigmoid(g_ref[bb]) for bb in group]
    lg_r = [-jnp.exp(gpt_ref[0][:, 0:1]) * _softplus(gt_ref[bb] + gpt_ref[1][:, 0:1])
            for bb in group]

    ii = lax.broadcasted_iota(jnp.int32, (tb, tb), 0)
    jj = lax.broadcasted_iota(jnp.int32, (tb, tb), 1)
    same_chunk = _blk_id(ii, GDN_CHUNK) == _blk_id(jj, GDN_CHUNK)
    cum = jnp.logical_and(same_chunk, (ii <= jj) if reverse else (ii >= jj))
    cum_c = jnp.where(cum, 1.0, 0.0).astype(BF16)
    cum_r = jnp.where(jnp.logical_and(same_chunk, (jj <= ii) if reverse else (jj >= ii)),
                      1.0, 0.0).astype(BF16)
    ones_c = jnp.where(same_chunk, 1.0, 0.0).astype(BF16)
    gam_c = [_dot2_exact_lhs(cum_c, lg_c[bb]) for bb in group]
    tot_c = [_dot2_exact_lhs(ones_c, lg_c[bb]) for bb in group]
    gam_r = [_dot2_exact_rhs(lg_r[bb], cum_r) for bb in group]

    sc_rows = 2 * GDN_CHUNK
    pi = lax.broadcasted_iota(jnp.int32, (sc_rows, sc_rows), 0)
    pj = lax.broadcasted_iota(jnp.int32, (sc_rows, sc_rows), 1)
    tri_incl = (pi <= pj) if reverse else (pi >= pj)
    tri_strict = (pi < pj) if reverse else (pi > pj)

    def same(level):
        return _blk_id(pi, level) == _blk_id(pj, level)

    incl = jnp.logical_and(same(GDN_CHUNK), tri_incl)
    strict = jnp.logical_and(same(GDN_CHUNK), tri_strict)
    m16 = jnp.logical_and(strict, same(16))
    m32 = jnp.logical_and(jnp.logical_and(strict, same(32)), jnp.logical_not(same(16)))
    m64 = jnp.logical_and(strict, jnp.logical_not(same(32)))
    eye = jnp.where(pi == pj, 1.0, 0.0)

    n_sc = tb // sc_rows
    sc_order = range(n_sc - 1, -1, -1) if reverse else range(n_sc)
    ch_order = (1, 0) if reverse else (0, 1)

    heads = [(bb, h) for bb in group for h in range(nh)]
    tiles = [(u, sc) for u in heads for sc in range(n_sc)]
    q_all = {u: qkv_slab(u[0], u[1] * hd, 0) for u in heads}
    k_all = {u: qkv_slab(u[0], GDN_WIDTH + u[1] * hd, 1) for u in heads}
    v_all = {u: qkv_slab(u[0], 2 * GDN_WIDTH + u[1] * hd, 2) for u in heads}

    def rows(a, sc):
        return a[sc * sc_rows:(sc + 1) * sc_rows]

    def col_a(u):
        return 2 * nh * d_idx + u[1]

    q_t = {t: rows(q_all[t[0]], t[1]) for t in tiles}
    k_t = {t: rows(k_all[t[0]], t[1]) for t in tiles}
    v_t = {t: rows(v_all[t[0]], t[1]) for t in tiles}
    gcol = {(u, sc): rows(gam_c[u[0]], sc)[:, col_a(u):col_a(u) + 1] for u, sc in tiles}
    tcol = {(u, sc): rows(tot_c[u[0]], sc)[:, col_a(u):col_a(u) + 1] for u, sc in tiles}
    bcol = {(u, sc): rows(beta_c[u[0]], sc)[:, col_a(u) + nh:col_a(u) + nh + 1] for u, sc in tiles}
    grow = {(u, sc): gam_r[u[0]][col_a(u):col_a(u) + 1, sc * sc_rows:(sc + 1) * sc_rows]
            for u, sc in tiles}
    decay = {t: jnp.exp(jnp.where(incl, gcol[t] - grow[t], -jnp.inf)) for t in tiles}
    eg = {t: jnp.exp(gcol[t]) for t in tiles}
    kb = {t: k_t[t].astype(BF16) for t in tiles}
    qb = {t: q_t[t].astype(BF16) for t in tiles}
    kk = {t: _dot_nt(kb[t], kb[t]) for t in tiles}
    qk = {t: (_dot_nt(qb[t], kb[t]) * decay[t]).astype(BF16) for t in tiles}
    neg_l = {t: jnp.where(strict, -(bcol[t] * kk[t] * decay[t]), 0.0) for t in tiles}
    x1 = {t: jnp.where(m16, neg_l[t], 0.0).astype(BF16) for t in tiles}
    p = {t: eye + x1[t].astype(F32) for t in tiles}
    xp = x1
    for _ in range(3):
        xp = {t: _dot(xp[t], xp[t]).astype(BF16) for t in tiles}
        p = {t: p[t] + _dot(p[t].astype(BF16), xp[t]) for t in tiles}
    for mask in (m32, m64):
        pb = {t: p[t].astype(BF16) for t in tiles}
        pe = {t: _dot(pb[t], jnp.where(mask, neg_l[t], 0.0).astype(BF16)).astype(BF16) for t in tiles}
        p = {t: p[t] + _dot(pe[t], pb[t]) for t in tiles}
    rhs = {t: jnp.concatenate([bcol[t] * v_t[t], (bcol[t] * eg[t]) * k_t[t]], axis=1).astype(BF16)
           for t in tiles}
    sol = {t: _dot(p[t].astype(BF16), rhs[t]) for t in tiles}
    kd = {t: (k_t[t] * jnp.exp(tcol[t] - gcol[t])).astype(BF16) for t in tiles}
    wkqg = {t: jnp.concatenate([sol[t][:, hd:], q_t[t] * eg[t]], axis=0).astype(BF16) for t in tiles}

    s = {u: s_ref[u[0], u[1]] for u in heads}
    for sc in sc_order:
        w_parts = {u: [None, None] for u in heads}
        og_parts = {u: [None, None] for u in heads}
        for ch in ch_order:
            c0 = ch * GDN_CHUNK
            for h in heads:
                t = (h, sc)
                lhs = jnp.concatenate([wkqg[t][c0:c0 + GDN_CHUNK],
                                       wkqg[t][sc_rows + c0:sc_rows + c0 + GDN_CHUNK]], axis=0)
                prod = _dot(lhs, s[h].astype(BF16))
                w_ch = sol[t][c0:c0 + GDN_CHUNK, :hd] - prod[:GDN_CHUNK]
                og_parts[h][ch] = prod[GDN_CHUNK:]
                w_parts[h][ch] = w_ch
                gc = jnp.exp(tcol[t][c0:c0 + 1, :])
                s[h] = gc * s[h] + _dot_tn(kd[t][c0:c0 + GDN_CHUNK], w_ch.astype(BF16))
        for h in heads:
            w_full = jnp.concatenate(w_parts[h], axis=0).astype(BF16)
            o = jnp.concatenate(og_parts[h], axis=0) + _dot(qk[(h, sc)], w_full)
            o_ref[h[0], sc * sc_rows:(sc + 1) * sc_rows, h[1] * hd:(h[1] + 1) * hd] = o.astype(o_ref.dtype)
    for h in heads:
        s_ref[h[0], h[1]] = s[h]


def _gdn(qkv, gates, gates_t, conv_w, gp, gpt, n_ctx_blk, reverse):
    bsz, s, w3 = qkv.shape
    nb = s // TOK_BLK
    halo_per_blk = TOK_BLK // HALO_ROWS
    n_halo = s // HALO_ROWS

    def blk(i):
        return _gdn_block(i, nb, n_ctx_blk, reverse)

    grp = GDN_BATCH_GROUP if bsz % GDN_BATCH_GROUP == 0 else 1
    tok_qkv = pl.BlockSpec((grp, TOK_BLK, w3), lambda b, i: (b, blk(i), 0))
    tok_o = pl.BlockSpec((grp, TOK_BLK, GDN_WIDTH), lambda b, i: (b, blk(i), 0))
    gate_specs = [pl.BlockSpec((grp, TOK_BLK, LANES), lambda b, i: (b, blk(i), 0)),
                  pl.BlockSpec((grp, 16, TOK_BLK), lambda b, i: (b, 0, blk(i)))]
    par_specs = [pl.BlockSpec(gp.shape, lambda b, i: (0, 0)),
                 pl.BlockSpec(gpt.shape, lambda b, i: (0, 0, 0))]
    state = pltpu.VMEM((grp, GDN_HEADS, GDN_HEAD_DIM, GDN_HEAD_DIM), F32)
    o_shape = jax.ShapeDtypeStruct((bsz, s, GDN_WIDTH), BF16)
    body = functools.partial(_gdn_kernel, reverse=reverse, n_blk=nb, n_ctx_blk=n_ctx_blk)
    if reverse:
        return pl.pallas_call(
            body, grid=(bsz // grp, nb),
            in_specs=[tok_qkv] + gate_specs + par_specs,
            out_specs=tok_o, out_shape=o_shape, scratch_shapes=[state],
            compiler_params=_cparams(("parallel", "arbitrary")), name="gdn_bwd",
        )(qkv, gates, gates_t, gp, gpt)
    halo_specs = [
        pl.BlockSpec((grp, HALO_ROWS, w3),
                     lambda b, i: (b, jnp.maximum(blk(i) * halo_per_blk - 1, 0), 0)),
        pl.BlockSpec((grp, HALO_ROWS, w3),
                     lambda b, i: (b, jnp.minimum((blk(i) + 1) * halo_per_blk, n_halo - 1), 0))]
    return pl.pallas_call(
        body, grid=(bsz // grp, nb),
        in_specs=[tok_qkv] + halo_specs + gate_specs + [pl.BlockSpec(conv_w.shape, lambda b, i: (0, 0))]
        + par_specs,
        out_specs=[tok_o, tok_qkv],
        out_shape=[o_shape, jax.ShapeDtypeStruct((bsz, s, w3), BF16)],
        scratch_shapes=[pltpu.VMEM((grp, TOK_BLK + 16, w3), F32), state],
        compiler_params=_cparams(("parallel", "arbitrary")), name="gdn_fwd",
    )(qkv, qkv, qkv, gates, gates_t, conv_w, gp, gpt)


def _merge_kernel(*refs, alpha, n_experts):
    (x_ref, ug_ref, vg_ref, of_ref, ob_ref, zs_ref, sga_ref, sgb_ref, mods_ref,
     slg_ref, slb_ref, sw_ref, sbias_ref, nw_ref, wpa_ref, wpb_ref, wo_ref,
     l1g_ref, l1b_ref) = refs[:19]
    if n_experts:
        router_ref, x1_ref, h2_ref, rcol_ref, rrow_ref, cnt_ref = refs[19:]
    else:
        x1_ref, h2_ref = refs[19:]
    tb = TOK_BLK
    vn = (_norm(vg_ref[0].astype(F32)) * slg_ref[...] + slb_ref[...]).astype(BF16)
    gd = vn.shape[1] // SGU_GROUPS
    rows = []
    for c in range(tb // SGU_CHUNK):
        cols = []
        for g in range(SGU_GROUPS):
            cols.append(_dot(sw_ref[g], vn[c * SGU_CHUNK:(c + 1) * SGU_CHUNK, g * gd:(g + 1) * gd]))
        rows.append(jnp.concatenate(cols, axis=1) + sbias_ref[...])
    mixed = jnp.concatenate(rows, axis=0)
    oa = (ug_ref[0].astype(F32) * mixed).astype(BF16)
    osum = of_ref[0].astype(F32) + ob_ref[0].astype(F32)
    heads = []
    for h in range(GDN_HEADS):
        oh = osum[:, h * GDN_HEAD_DIM:(h + 1) * GDN_HEAD_DIM]
        heads.append(oh * lax.rsqrt(jnp.mean(oh * oh, axis=-1, keepdims=True) + LN_EPS) * nw_ref[...])
    ogd = (jnp.concatenate(heads, axis=1) * zs_ref[0].astype(F32)).astype(BF16)
    y = (sga_ref[0].astype(F32) * _dot(oa, wpa_ref[...])
         + sgb_ref[0].astype(F32) * _dot(ogd, wpb_ref[...]))
    t = _dot(y.astype(BF16), wo_ref[...])
    x1 = _norm(alpha * x_ref[0] + mods_ref[0, 0, 2:3, :] * t) * l1g_ref[...] + l1b_ref[...]
    x1_ref[0] = x1
    h2 = _norm(x1) * (1.0 + mods_ref[0, 0, 4:5, :]) + mods_ref[0, 0, 3:4, :]
    h2_ref[0] = h2.astype(BF16)
    if n_experts:
        logits = _dot3(h2, router_ref[...])
        lane = lax.broadcasted_iota(jnp.int32, logits.shape, 1).astype(F32)
        lg = jnp.where(lane < n_experts, logits, -jnp.inf)
        m1 = jnp.max(lg, axis=-1, keepdims=True)
        i1 = jnp.min(jnp.where(lg == m1, lane, float(LANES)), axis=-1, keepdims=True)
        lg2 = jnp.where(lane == i1, -jnp.inf, lg)
        m2 = jnp.max(lg2, axis=-1, keepdims=True)
        i2 = jnp.min(jnp.where(lg2 == m2, lane, float(LANES)), axis=-1, keepdims=True)
        e2 = jnp.exp(m2 - m1)
        den = 1.0 + e2
        gate = jnp.where(lane == i1, 1.0 / den, 0.0) + jnp.where(lane == i2, e2 / den, 0.0)
        sel = jnp.where(jnp.logical_or(lane == i1, lane == i2), 1.0, 0.0)
        selb = sel.astype(BF16)
        ti = lax.broadcasted_iota(jnp.int32, (tb, tb), 0)
        tj = lax.broadcasted_iota(jnp.int32, (tb, tb), 1)
        before = jnp.where(tj < ti, 1.0, 0.0).astype(BF16)
        rcol_ref[0] = jnp.where(sel > 0.0, _dot(before, selb), MOE_NOT_SELECTED)
        er = lax.broadcasted_iota(jnp.int32, (8, LANES), 0)
        ec = lax.broadcasted_iota(jnp.int32, (8, LANES), 1)
        eye8 = jnp.where(er == ec, 1.0, 0.0).astype(BF16)
        sel_t = _dot_nt(eye8, selb)
        after = jnp.where(ti < tj, 1.0, 0.0).astype(BF16)
        rrow_ref[0, 0, 0:8, :] = jnp.where(sel_t > 0.0, _dot(sel_t.astype(BF16), after), MOE_NOT_SELECTED)
        g_hi = gate.astype(BF16)
        g_mid = (gate - g_hi.astype(F32)).astype(BF16)
        g_lo = (gate - g_hi.astype(F32) - g_mid.astype(F32)).astype(BF16)
        rrow_ref[0, 0, 8:16, :] = _dot_nt(eye8, g_hi) + (_dot_nt(eye8, g_mid) + _dot_nt(eye8, g_lo))
        cnt_ref[0, 0] = _dot(jnp.ones((8, tb), BF16), selb)


def _merge(x, ug, vg, o_f, o_b, zs, sga, sgb, mods, sgu_ln_g, sgu_ln_b, sgu_w, sgu_bias, norm_w,
           w_pa, w_pb, w_o, ln_g, ln_b, n_ctx_blk, alpha, router=None):
    bsz, s, d = x.shape
    nb = s // TOK_BLK
    sw = d // 2
    n_experts = 0 if router is None else router[1]

    def tok(width):
        return pl.BlockSpec((1, TOK_BLK, width), lambda b, i: (b, i, 0))

    def full(a):
        return pl.BlockSpec(a.shape, lambda b, i: (0,) * a.ndim)

    consts = [sgu_ln_g.reshape(1, sw), sgu_ln_b.reshape(1, sw), sgu_w, sgu_bias,
              norm_w.reshape(1, GDN_HEAD_DIM), w_pa, w_pb, w_o, ln_g.reshape(1, d), ln_b.reshape(1, d)]
    if n_experts:
        consts.append(router[0])
    in_specs = [tok(d), tok(sw), tok(sw), tok(GDN_WIDTH), tok(GDN_WIDTH), tok(GDN_WIDTH), tok(d), tok(d),
                pl.BlockSpec((1, 1, 6, d), lambda b, i: (b, jnp.where(i < n_ctx_blk, 0, 1), 0, 0))]
    in_specs += [full(a) for a in consts]
    out_specs = [tok(d), tok(d)]
    out_shape = [jax.ShapeDtypeStruct((bsz, s, d), F32), jax.ShapeDtypeStruct((bsz, s, d), BF16)]
    if n_experts:
        assert n_experts <= 8
        out_specs += [tok(LANES),
                      pl.BlockSpec((1, 1, 16, TOK_BLK), lambda b, i: (b, i, 0, 0)),
                      pl.BlockSpec((1, 1, 8, LANES), lambda b, i: (b, i, 0, 0))]
        out_shape += [jax.ShapeDtypeStruct((bsz, s, LANES), F32),
                      jax.ShapeDtypeStruct((bsz, nb, 16, TOK_BLK), F32),
                      jax.ShapeDtypeStruct((bsz, nb, 8, LANES), F32)]
    return pl.pallas_call(
        functools.partial(_merge_kernel, alpha=alpha, n_experts=n_experts),
        grid=(bsz, nb),
        in_specs=in_specs,
        out_specs=out_specs,
        out_shape=out_shape,
        compiler_params=_cparams(("parallel", "arbitrary")),
        name="merge",
    )(x, ug, vg, o_f, o_b, zs, sga, sgb, mods, *consts)


def _ffn_kernel(h_ref, w1_ref, w3_ref, w2_ref, o_ref, acc_ref):
    f = pl.program_id(1)

    @pl.when(f == 0)
    def _():
        acc_ref[...] = jnp.zeros_like(acc_ref)

    h = h_ref[...]
    mid = (_silu(_dot(h, w1_ref[...].astype(BF16))) * _dot(h, w3_ref[...].astype(BF16))).astype(BF16)
    acc_ref[...] += _dot(mid, w2_ref[...].astype(BF16))

    @pl.when(f == pl.num_programs(1) - 1)
    def _():
        o_ref[...] = acc_ref[...].astype(o_ref.dtype)


def _ffn(h, w1, w3, w2, tm, tf):
    t, d = h.shape
    ff = w1.shape[1]
    return pl.pallas_call(
        _ffn_kernel,
        grid=(t // tm, ff // tf),
        in_specs=[pl.BlockSpec((tm, d), lambda i, f: (i, 0)),
                  pl.BlockSpec((d, tf), lambda i, f: (0, f)),
                  pl.BlockSpec((d, tf), lambda i, f: (0, f)),
                  pl.BlockSpec((tf, d), lambda i, f: (f, 0))],
        out_specs=pl.BlockSpec((tm, d), lambda i, f: (i, 0)),
        out_shape=jax.ShapeDtypeStruct((t, d), BF16),
        scratch_shapes=[pltpu.VMEM((tm, d), F32)],
        compiler_params=_cparams(("parallel", "arbitrary")),
        name="ffn",
    )(h, w1, w3, w2)


MOE_GATHER_ROWS = 128
MOE_FFN_ROWS = 512
MOE_ROW_ALIGN = 16
MOE_NOT_SELECTED = -4096.0
MOE_TOKEN_TILE = 3072
FFN_TOKEN_TILE = 1536


def _moe_rows_capacity(tm):
    rows = tm + TOK_BLK + LANES
    return -(-rows // MOE_FFN_ROWS) * MOE_FFN_ROWS


def _moe_kernel(cnt_ref, off_ref, tot_ref, h_ref, rcol_ref, rrow_ref, w1_ref, w3_ref, w2_ref,
                o_ref, xs_ref, ys_ref, wrow_ref, *, n_experts):
    i = pl.program_id(0)
    e = pl.program_id(1)
    f = pl.program_id(2)
    tm = h_ref.shape[0]
    n_sub = tm // TOK_BLK
    gr = MOE_GATHER_ROWS
    fr = MOE_FFN_ROWS
    tot = tot_ref[i * n_experts + e]
    quarter = fr // 4
    n_quarters = _blk_id(tot + (quarter - 1), quarter)
    n_full = _blk_id(n_quarters, 4)
    has_half = jnp.bitwise_and(n_quarters, 2) != 0
    has_quarter = jnp.bitwise_and(n_quarters, 1) != 0

    def piece(j):
        base = (i * n_sub + j) * n_experts + e
        off = off_ref[base]
        start = pl.multiple_of(_blk_id(off, MOE_ROW_ALIGN) * MOE_ROW_ALIGN, MOE_ROW_ALIGN)
        return start, off - start, cnt_ref[base]

    @pl.when(jnp.logical_and(e == 0, f == 0))
    def _():
        o_ref[...] = jnp.zeros_like(o_ref)
        xs_ref[...] = jnp.zeros_like(xs_ref)
        ys_ref[...] = jnp.zeros_like(ys_ref)
        wrow_ref[...] = jnp.zeros_like(wrow_ref)

    @pl.when(f == 0)
    def _():
        def clear(rb, carry):
            rows = pl.ds(pl.multiple_of(rb * fr, fr), fr)
            xs_ref[rows, :] = jnp.zeros((fr, xs_ref.shape[1]), BF16)
            ys_ref[rows, :] = jnp.zeros((fr, ys_ref.shape[1]), F32)
            wrow_ref[rows, :] = jnp.zeros((fr, LANES), F32)
            return carry

        lax.fori_loop(0, _blk_id(tot + (fr - 1), fr) + 1, clear, 0)

        row_id = lax.broadcasted_iota(jnp.int32, (gr, TOK_BLK), 0).astype(F32)
        for j in range(n_sub):
            start, shift, cnt = piece(j)
            rank = rrow_ref[j, pl.ds(e, 1), :]
            gate = rrow_ref[j, pl.ds(8 + e, 1), :]
            row0 = row_id - shift.astype(F32)

            def gather(b, carry, j=j, start=start, rank=rank, gate=gate, row0=row0):
                hit = rank == row0 + (b * gr).astype(F32)
                rows = _dot(jnp.where(hit, 1.0, 0.0).astype(BF16), h_ref[j * TOK_BLK:(j + 1) * TOK_BLK, :])
                dst = pl.ds(pl.multiple_of(start + b * gr, MOE_ROW_ALIGN), gr)
                xs_ref[dst, :] = xs_ref[dst, :] + rows.astype(BF16)
                wgt = jnp.sum(jnp.where(hit, gate, 0.0), axis=-1, keepdims=True)
                wrow_ref[dst, :] = wrow_ref[dst, :] + jnp.broadcast_to(wgt, (gr, LANES))
                return carry

            lax.fori_loop(0, _blk_id(shift + cnt + (gr - 1), gr), gather, 0)

    def ffn_rows(r0, n):
        x = xs_ref[pl.ds(r0, n), :]
        mid = _silu(_dot(x, w1_ref[0])) * _dot(x, w3_ref[0]) * wrow_ref[pl.ds(r0, n), :][:, 0:1]
        ys_ref[pl.ds(r0, n), :] += _dot(mid.astype(BF16), w2_ref[0])

    def ffn(rb, carry):
        ffn_rows(pl.multiple_of(rb * fr, fr), fr)
        return carry

    lax.fori_loop(0, n_full, ffn, 0)

    @pl.when(has_half)
    def _():
        ffn_rows(pl.multiple_of(n_full * fr, fr), 2 * quarter)

    @pl.when(has_quarter)
    def _():
        r0 = n_full * fr + jnp.where(has_half, 2 * quarter, 0)
        ffn_rows(pl.multiple_of(r0, quarter), quarter)

    @pl.when(f == pl.num_programs(2) - 1)
    def _():
        col_id = lax.broadcasted_iota(jnp.int32, (TOK_BLK, LANES), 1).astype(F32)
        lane = lax.broadcasted_iota(jnp.int32, (TOK_BLK, LANES), 1)
        for j in range(n_sub):
            start, shift, cnt = piece(j)
            tok = slice(j * TOK_BLK, (j + 1) * TOK_BLK)

            def rank_col(tok=tok):
                return jnp.sum(jnp.where(lane == e, rcol_ref[tok, :], 0.0), axis=-1, keepdims=True)

            for w in range(TOK_BLK // LANES + 1):

                @pl.when(jnp.logical_and(cnt > 0, shift + cnt > w * LANES))
                def _(w=w, tok=tok, start=start, shift=shift, rank_col=rank_col):
                    cols = col_id - (shift - w * LANES).astype(F32)
                    onehot = jnp.where(rank_col() == cols, 1.0, 0.0).astype(BF16)
                    rows = pl.ds(pl.multiple_of(start + w * LANES, MOE_ROW_ALIGN), LANES)
                    o_ref[tok, :] += _dot(onehot, ys_ref[rows, :].astype(BF16))


def _moe(h, rcol, rrow, cnt, off, tot, w1, w3, w2, tm, tf):
    t, d = h.shape
    ne, _, ff = w1.shape
    n_sub = tm // TOK_BLK
    cap = _moe_rows_capacity(tm)
    once = pl.Buffered(1)
    grid_spec = pltpu.PrefetchScalarGridSpec(
        num_scalar_prefetch=3,
        grid=(t // tm, ne, ff // tf),
        in_specs=[pl.BlockSpec((tm, d), lambda i, e, f, *_: (i, 0), pipeline_mode=once),
                  pl.BlockSpec((tm, LANES), lambda i, e, f, *_: (i, 0), pipeline_mode=once),
                  pl.BlockSpec((n_sub, 16, TOK_BLK), lambda i, e, f, *_: (i, 0, 0)),
                  pl.BlockSpec((1, d, tf), lambda i, e, f, *_: (e, 0, f)),
                  pl.BlockSpec((1, d, tf), lambda i, e, f, *_: (e, 0, f)),
                  pl.BlockSpec((1, tf, d), lambda i, e, f, *_: (e, f, 0))],
        out_specs=pl.BlockSpec((tm, d), lambda i, e, f, *_: (i, 0), pipeline_mode=once),
        scratch_shapes=[pltpu.VMEM((cap, d), BF16), pltpu.VMEM((cap, d), F32),
                        pltpu.VMEM((cap, LANES), F32)],
    )
    return pl.pallas_call(
        functools.partial(_moe_kernel, n_experts=ne),
        grid_spec=grid_spec,
        out_shape=jax.ShapeDtypeStruct((t, d), F32),
        compiler_params=_cparams(("parallel", "arbitrary", "arbitrary")),
        name="moe",
    )(cnt, off, tot, h, rcol, rrow, w1, w3, w2)


def _final_kernel(x_ref, f_ref, mods_ref, g_ref, b_ref, o_ref, *, alpha):
    x = alpha * x_ref[0] + mods_ref[0, 0, 5:6, :] * f_ref[0].astype(F32)
    o_ref[0] = _norm(x) * g_ref[...] + b_ref[...]


def _final(x1, f, mods, ln_g, ln_b, n_ctx_blk, alpha):
    bsz, s, d = x1.shape
    nb = s // TOK_BLK - n_ctx_blk
    return pl.pallas_call(
        functools.partial(_final_kernel, alpha=alpha),
        grid=(bsz, nb),
        in_specs=[pl.BlockSpec((1, TOK_BLK, d), lambda b, i: (b, i + n_ctx_blk, 0)),
                  pl.BlockSpec((1, TOK_BLK, d), lambda b, i: (b, i + n_ctx_blk, 0)),
                  pl.BlockSpec((1, 1, 6, d), lambda b, i: (b, 1, 0, 0)),
                  pl.BlockSpec((1, d), lambda b, i: (0, 0)),
                  pl.BlockSpec((1, d), lambda b, i: (0, 0))],
        out_specs=pl.BlockSpec((1, TOK_BLK, d), lambda b, i: (b, i, 0)),
        out_shape=jax.ShapeDtypeStruct((bsz, nb * TOK_BLK, d), F32),
        compiler_params=_cparams(("parallel", "arbitrary")),
        name="final",
    )(x1, f, mods, ln_g.reshape(1, d), ln_b.reshape(1, d))


def _token_tile(t, target):
    best = TOK_BLK
    for m in range(1, t // TOK_BLK + 1):
        cand = m * TOK_BLK
        if cand <= target and t % cand == 0:
            best = cand
    return best


def _ff_tile(ff, target):
    best = LANES
    for m in range(1, ff // LANES + 1):
        cand = m * LANES
        if cand <= target and ff % cand == 0:
            best = cand
    return best


def kernel(x, c, ctx, c_ctx, w_ada, b_ada, w_in, sgu_ln_g, sgu_ln_b, sgu_w, sgu_b, conv_w, a_log, dt_bias, gdn_norm_w, w_pa, w_pb, w_o, ln1_g, ln1_b, ln2_g, ln2_b, ffn_w1, ffn_w3, ffn_w2, moe_router, moe_w1, moe_w3, moe_w2):
    bsz, seq, d = x.shape
    ctx_len = ctx.shape[1]
    depth = w_ada.shape[0]
    alpha = (2.0 * depth) ** 0.25
    s = ctx_len + seq
    assert ctx_len % TOK_BLK == 0 and seq % TOK_BLK == 0
    assert d // 2 == SGU_GROUPS * LANES and bsz < 8
    n_ctx_blk = ctx_len // TOK_BLK
    sw = d // 2
    gw = GDN_WIDTH
    nh = GDN_HEADS

    cond = jnp.zeros((8, d), F32).at[:bsz].set(c).at[bsz].set(c_ctx)

    t_all = bsz * s
    tm_ffn = _token_tile(t_all, FFN_TOKEN_TILE)

    f_prev = None
    prev = None
    x_cur = x
    for l in range(depth):
        mod = _ada(cond, w_ada[l], b_ada[l]).reshape(8, 6, d)
        mods = jnp.stack([jnp.broadcast_to(mod[bsz], (bsz, 6, d)), mod[:bsz]], axis=1)

        g0 = 2 * sw + 4 * gw
        w_main = jnp.concatenate([w_in[l][:, :g0], w_in[l][:, g0 + 4 * nh:]], axis=1).astype(BF16)
        w_gate = jnp.pad(w_in[l][:, g0:g0 + 4 * nh], ((0, 0), (0, LANES - 4 * nh))).astype(BF16)
        outs = _inproj(x_cur, w_main, w_gate, mods, n_ctx_blk, prev=prev,
                       ctx=ctx if prev is None else None, alpha=alpha)
        x_cur, ug, vg, qkv, zs, sga, sgb, gates = outs

        gates_t = jnp.swapaxes(gates[:, :, :4 * nh], 1, 2)
        al = jnp.zeros((LANES,), F32)
        db = jnp.zeros((LANES,), F32)
        for dd in range(2):
            al = al.at[2 * nh * dd:2 * nh * dd + nh].set(a_log[l, dd])
            db = db.at[2 * nh * dd:2 * nh * dd + nh].set(dt_bias[l, dd])
        gp = jnp.zeros((8, LANES), F32).at[0].set(al).at[1].set(db)
        gpt = jnp.stack([jnp.broadcast_to(al[:16, None], (16, LANES)),
                         jnp.broadcast_to(db[:16, None], (16, LANES))])
        cw = jnp.pad(conv_w[l], ((0, 8 - CONV_K), (0, 0)))
        o_f, qkvn = _gdn(qkv, gates, gates_t, cw, gp, gpt, n_ctx_blk, reverse=False)
        o_b = _gdn(qkvn, gates, gates_t, None, gp, gpt, n_ctx_blk, reverse=True)

        sgu_bias = jnp.repeat(sgu_b[l].T, LANES, axis=1)
        is_moe = l % 2 == 1
        router = None
        if is_moe:
            r = moe_router[l // 2]
            router = (jnp.pad(r, ((0, 0), (0, LANES - r.shape[1]))), r.shape[1])
        outs = _merge(x_cur, ug, vg, o_f, o_b, zs, sga, sgb, mods, sgu_ln_g[l], sgu_ln_b[l],
                      sgu_w[l].astype(BF16), sgu_bias, gdn_norm_w[l], w_pa[l].astype(BF16),
                      w_pb[l].astype(BF16), w_o[l].astype(BF16), ln1_g[l], ln1_b[l], n_ctx_blk, alpha,
                      router=router)
        x1, h2 = outs[0], outs[1]

        h2f = h2.reshape(t_all, d)
        if is_moe:
            ne = moe_router.shape[-1]
            rcol = outs[2].reshape(t_all, LANES)
            rrow = outs[3].reshape(t_all // TOK_BLK, 16, TOK_BLK)
            tm = _token_tile(t_all, MOE_TOKEN_TILE)
            n_sub = tm // TOK_BLK
            cnt = outs[4][:, :, 0, :ne].astype(jnp.int32).reshape(t_all // tm, n_sub, ne)
            off = jnp.cumsum(cnt, axis=1) - cnt
            tot = jnp.sum(cnt, axis=1)
            tf = _ff_tile(moe_w1.shape[-1], 512)
            f = _moe(h2f, rcol, rrow, cnt.reshape(-1), off.reshape(-1), tot.reshape(-1),
                     moe_w1[l // 2].astype(BF16), moe_w3[l // 2].astype(BF16),
                     moe_w2[l // 2].astype(BF16), tm, tf)
        else:
            tf = _ff_tile(ffn_w1.shape[-1], 512)
            f = _ffn(h2f, ffn_w1[l // 2], ffn_w3[l // 2], ffn_w2[l // 2], tm_ffn, tf)
        f = f.reshape(bsz, s, d)
        prev = (f, mods, ln2_g[l], ln2_b[l])
        f_prev, mods_prev = f, mods
        x_cur = x1

    return _final(x_cur, f_prev, mods_prev, ln2_g[depth - 1], ln2_b[depth - 1], n_ctx_blk, alpha)
```

```python
import functools
import math

import jax
import jax.numpy as jnp
from jax import lax
from jax.experimental import pallas as pl
from jax.experimental.pallas import tpu as pltpu

F32 = jnp.float32
BF16 = jnp.bfloat16

LN_EPS = 1e-6
SGU_CHUNK = 128
SGU_GROUPS = 4
GDN_HEADS = 4
GDN_HEAD_DIM = 128
GDN_WIDTH = GDN_HEADS * GDN_HEAD_DIM
GDN_CHUNK = 64
CONV_K = 5
TOP_K = 2

LANES = 128
TOK_BLK = 256
HALO_ROWS = 16
GDN_BATCH_GROUP = 4
VMEM_LIMIT = 56 * 1024 * 1024


def _cparams(sem):
    return pltpu.CompilerParams(dimension_semantics=sem, vmem_limit_bytes=VMEM_LIMIT)


def _dot(a, b):
    return jnp.dot(a, b, preferred_element_type=F32)


def _dot_nt(a, b):
    return lax.dot_general(a, b, (((1,), (1,)), ((), ())), preferred_element_type=F32)


def _dot_tn(a, b):
    return lax.dot_general(a, b, (((0,), (0,)), ((), ())), preferred_element_type=F32)


def _split(a):
    hi = a.astype(BF16)
    lo = (a - hi.astype(F32)).astype(BF16)
    return hi, lo


def _dot3(a, b):
    ah, al = _split(a)
    bh, bl = _split(b)
    return _dot(ah, bh) + (_dot(ah, bl) + _dot(al, bh))


def _dot2_exact_lhs(a_bf, b):
    bh, bl = _split(b)
    return _dot(a_bf, bh) + _dot(a_bf, bl)


def _dot2_exact_rhs(a, b_bf):
    ah, al = _split(a)
    return _dot(ah, b_bf) + _dot(al, b_bf)


def _norm(x):
    mu = jnp.mean(x, axis=-1, keepdims=True)
    xc = x - mu
    var = jnp.mean(xc * xc, axis=-1, keepdims=True)
    return xc * lax.rsqrt(var + LN_EPS)


def _sigmoid(x):
    return 1.0 / (1.0 + jnp.exp(-x))


def _silu(x):
    return x * _sigmoid(x)


def _gelu(x):
    return 0.5 * x * (1.0 + lax.erf(x * (2.0 ** -0.5)))


def _blk_id(idx, size):
    return jnp.right_shift(idx, int(math.log2(size)))


def _softplus(x):
    return jnp.maximum(x, 0.0) + jnp.log1p(jnp.exp(-jnp.abs(x)))


def _ada_kernel(c_ref, w_ref, b_ref, o_ref):
    o_ref[...] = _dot3(_silu(c_ref[...]), w_ref[...]) + b_ref[...]


def _ada(cond, w, b):
    rows, d = cond.shape
    n = w.shape[1]
    tn = n // 6
    return pl.pallas_call(
        _ada_kernel,
        grid=(n // tn,),
        in_specs=[pl.BlockSpec((rows, d), lambda j: (0, 0)),
                  pl.BlockSpec((d, tn), lambda j: (0, j)),
                  pl.BlockSpec((1, tn), lambda j: (0, j))],
        out_specs=pl.BlockSpec((rows, tn), lambda j: (0, j)),
        out_shape=jax.ShapeDtypeStruct((rows, n), F32),
        compiler_params=_cparams(("arbitrary",)),
        name="ada",
    )(cond, w, b.reshape(1, n))


def _inproj_kernel(*refs, has_prev, alpha, seg, n_ctx_blk):
    if has_prev:
        (x_ref, f_ref, mprev_ref, lng_ref, lnb_ref, mods_ref, w_ref, wg_ref, wgt_ref,
         xo_ref, ug_ref, vg_ref, qkv_ref, zs_ref, sga_ref, sgb_ref, gates_ref, gates_t_ref) = refs
        x = alpha * x_ref[0] + mprev_ref[0, 0, 5:6, :] * f_ref[0].astype(F32)
        x = _norm(x) * lng_ref[...] + lnb_ref[...]
        xo_ref[0] = x
    else:
        (c_ref, x_ref, mods_ref, w_ref, wg_ref, wgt_ref,
         xo_ref, ug_ref, vg_ref, qkv_ref, zs_ref, sga_ref, sgb_ref, gates_ref, gates_t_ref) = refs
        x = jnp.where(pl.program_id(1) < n_ctx_blk, c_ref[0], x_ref[0])
        xo_ref[0] = x
    h = _norm(x) * (1.0 + mods_ref[0, 0, 1:2, :]) + mods_ref[0, 0, 0:1, :]
    hb = h.astype(BF16)

    def proj(lo, hi):
        return _dot(hb, w_ref[:, lo:hi])

    sw, gw = seg
    c0 = 0
    ug_ref[0] = _gelu(proj(c0, c0 + sw)).astype(BF16)
    c0 += sw
    vg_ref[0] = _gelu(proj(c0, c0 + sw)).astype(BF16)
    c0 += sw
    for j in range(3):
        qkv_ref[0, :, j * gw:(j + 1) * gw] = proj(c0, c0 + gw).astype(BF16)
        c0 += gw
    zs_ref[0] = _silu(proj(c0, c0 + gw)).astype(BF16)
    c0 += gw
    d = sga_ref.shape[-1]
    half = d // 2
    for j in range(2):
        sga_ref[0, :, j * half:(j + 1) * half] = _sigmoid(proj(c0, c0 + half)).astype(BF16)
        c0 += half
    for j in range(2):
        sgb_ref[0, :, j * half:(j + 1) * half] = _sigmoid(proj(c0, c0 + half)).astype(BF16)
        c0 += half
    gates_ref[0] = _dot(hb, wg_ref[...])
    gates_t_ref[0] = _dot_nt(wgt_ref[...], hb)


def _inproj(x, w_main, w_gate, w_gate_t, mods, n_ctx_blk, prev=None, ctx=None, alpha=1.0):
    bsz, s, d = x.shape
    if ctx is not None:
        s += ctx.shape[1]
    nb = s // TOK_BLK
    sw = d // 2
    gw = GDN_WIDTH
    n_main = w_main.shape[1]

    def tok(width):
        return pl.BlockSpec((1, TOK_BLK, width), lambda b, i: (b, i, 0))

    def mod_spec():
        return pl.BlockSpec((1, 1, 6, d), lambda b, i: (b, jnp.where(i < n_ctx_blk, 0, 1), 0, 0))

    def full(shape):
        return pl.BlockSpec(shape, lambda b, i: (0,) * len(shape))

    if ctx is not None:
        in_specs = [pl.BlockSpec((1, TOK_BLK, d), lambda b, i: (b, jnp.minimum(i, n_ctx_blk - 1), 0)),
                    pl.BlockSpec((1, TOK_BLK, d), lambda b, i: (b, jnp.maximum(i - n_ctx_blk, 0), 0))]
        args = [ctx, x]
    else:
        in_specs = [tok(d)]
        args = [x]
    if prev is not None:
        f_prev, mods_prev, ln_g, ln_b = prev
        in_specs += [tok(d), mod_spec(), full((1, d)), full((1, d))]
        args += [f_prev, mods_prev, ln_g.reshape(1, d), ln_b.reshape(1, d)]
    in_specs += [mod_spec(), full((d, n_main)), full((d, LANES)), full(w_gate_t.shape)]
    args += [mods, w_main, w_gate, w_gate_t]

    out_specs = []
    out_shape = []
    out_specs.append(tok(d))
    out_shape.append(jax.ShapeDtypeStruct((bsz, s, d), F32))
    for width, dt in ((sw, BF16), (sw, BF16), (3 * gw, BF16), (gw, BF16), (d, BF16), (d, BF16),
                      (LANES, F32)):
        out_specs.append(tok(width))
        out_shape.append(jax.ShapeDtypeStruct((bsz, s, width), dt))
    out_specs.append(pl.BlockSpec((1, w_gate_t.shape[0], TOK_BLK), lambda b, i: (b, 0, i)))
    out_shape.append(jax.ShapeDtypeStruct((bsz, w_gate_t.shape[0], s), F32))

    return pl.pallas_call(
        functools.partial(_inproj_kernel, has_prev=prev is not None, alpha=alpha, seg=(sw, gw),
                          n_ctx_blk=n_ctx_blk),
        grid=(bsz, nb),
        in_specs=in_specs,
        out_specs=out_specs,
        out_shape=out_shape,
        compiler_params=_cparams(("parallel", "arbitrary")),
        name="inproj",
    )(*args)


def _gdn_block(i, n_blk, n_ctx_blk, reverse):
    if not reverse:
        return i
    return jnp.where(i < n_ctx_blk, n_ctx_blk - 1 - i, n_blk - 1 - (i - n_ctx_blk))


def _gdn_kernel(*refs, reverse, n_blk, n_ctx_blk):
    if reverse:
        qkvn_ref, g_ref, gt_ref, gp_ref, gpt_ref, o_ref, s_ref = refs
    else:
        (qkv_ref, prev_ref, next_ref, g_ref, gt_ref, cw_ref, gp_ref, gpt_ref,
         o_ref, qkvn_ref, win_ref, s_ref) = refs
    i = pl.program_id(1)
    blk = _gdn_block(i, n_blk, n_ctx_blk, reverse)
    tb = TOK_BLK
    hd = GDN_HEAD_DIM
    nh = GDN_HEADS
    d_idx = 1 if reverse else 0
    group = range(o_ref.shape[0])

    @pl.when(i == 0)
    def _():
        s_ref[...] = jnp.zeros_like(s_ref)

    if not reverse:
        first = jnp.logical_or(blk == 0, blk == n_ctx_blk)
        last = jnp.logical_or(blk == n_ctx_blk - 1, blk == n_blk - 1)
        for bb in group:
            prev = prev_ref[bb].astype(F32)[HALO_ROWS - 8:, :]
            nxt = next_ref[bb].astype(F32)[:8, :]
            win_ref[bb, 0:8, :] = jnp.where(first, 0.0, prev)
            win_ref[bb, 8:8 + tb, :] = qkv_ref[bb].astype(F32)
            win_ref[bb, 8 + tb:16 + tb, :] = jnp.where(last, 0.0, nxt)

    def conv_slab(bb, col):
        win = win_ref[bb, :, col:col + hd]
        n = win.shape[0]
        acc = None
        for j in range(CONV_K):
            term = cw_ref[j:j + 1, col:col + hd] * win
            shift = (CONV_K // 2 - j) % n
            if shift:
                term = pltpu.roll(term, shift, axis=0)
            acc = term if acc is None else acc + term
        return _silu(acc[8:8 + tb])

    def l2n(x):
        return x * lax.rsqrt(jnp.sum(x * x, axis=-1, keepdims=True) + LN_EPS)

    def qkv_slab(bb, col, kind):
        if reverse:
            return qkvn_ref[bb, :, col:col + hd].astype(F32)
        x = conv_slab(bb, col)
        if kind == 0:
            x = l2n(x) * (hd ** -0.5)
        elif kind == 1:
            x = l2n(x)
        qkvn_ref[bb, :, col:col + hd] = x.astype(BF16)
        return x

    lg_c = [-jnp.exp(gp_ref[0:1, :]) * _softplus(g_ref[bb] + gp_ref[1:2, :]) for bb in group]
    beta_c = [_sigmoid(g_ref[bb]) for bb in group]
    lg_r = [-jnp.exp(gpt_ref[0][:, 0:1]) * _softplus(gt_ref[bb] + gpt_ref[1][:, 0:1])
            for bb in group]

    ii = lax.broadcasted_iota(jnp.int32, (tb, tb), 0)
    jj = lax.broadcasted_iota(jnp.int32, (tb, tb), 1)
    same_chunk = _blk_id(ii, GDN_CHUNK) == _blk_id(jj, GDN_CHUNK)
    cum = jnp.logical_and(same_chunk, (ii <= jj) if reverse else (ii >= jj))
    cum_c = jnp.where(cum, 1.0, 0.0).astype(BF16)
    cum_r = jnp.where(jnp.logical_and(same_chunk, (jj <= ii) if reverse else (jj >= ii)),
                      1.0, 0.0).astype(BF16)
    ones_c = jnp.where(same_chunk, 1.0, 0.0).astype(BF16)
    gam_c = [_dot2_exact_lhs(cum_c, lg_c[bb]) for bb in group]
    tot_c = [_dot2_exact_lhs(ones_c, lg_c[bb]) for bb in group]
    gam_r = [_dot2_exact_rhs(lg_r[bb], cum_r) for bb in group]

    sc_rows = 2 * GDN_CHUNK
    pi = lax.broadcasted_iota(jnp.int32, (sc_rows, sc_rows), 0)
    pj = lax.broadcasted_iota(jnp.int32, (sc_rows, sc_rows), 1)
    tri_incl = (pi <= pj) if reverse else (pi >= pj)
    tri_strict = (pi < pj) if reverse else (pi > pj)

    def same(level):
        return _blk_id(pi, level) == _blk_id(pj, level)

    incl = jnp.logical_and(same(GDN_CHUNK), tri_incl)
    strict = jnp.logical_and(same(GDN_CHUNK), tri_strict)
    m16 = jnp.logical_and(strict, same(16))
    m32 = jnp.logical_and(jnp.logical_and(strict, same(32)), jnp.logical_not(same(16)))
    m64 = jnp.logical_and(strict, jnp.logical_not(same(32)))
    eye = jnp.where(pi == pj, 1.0, 0.0)

    n_sc = tb // sc_rows
    sc_order = range(n_sc - 1, -1, -1) if reverse else range(n_sc)
    ch_order = (1, 0) if reverse else (0, 1)

    heads = [(bb, h) for bb in group for h in range(nh)]
    tiles = [(u, sc) for u in heads for sc in range(n_sc)]
    q_all = {u: qkv_slab(u[0], u[1] * hd, 0) for u in heads}
    k_all = {u: qkv_slab(u[0], GDN_WIDTH + u[1] * hd, 1) for u in heads}
    v_all = {u: qkv_slab(u[0], 2 * GDN_WIDTH + u[1] * hd, 2) for u in heads}

    def rows(a, sc):
        return a[sc * sc_rows:(sc + 1) * sc_rows]

    def col_a(u):
        return 2 * nh * d_idx + u[1]

    q_t = {t: rows(q_all[t[0]], t[1]) for t in tiles}
    k_t = {t: rows(k_all[t[0]], t[1]) for t in tiles}
    v_t = {t: rows(v_all[t[0]], t[1]) for t in tiles}
    gcol = {(u, sc): rows(gam_c[u[0]], sc)[:, col_a(u):col_a(u) + 1] for u, sc in tiles}
    tcol = {(u, sc): rows(tot_c[u[0]], sc)[:, col_a(u):col_a(u) + 1] for u, sc in tiles}
    bcol = {(u, sc): rows(beta_c[u[0]], sc)[:, col_a(u) + nh:col_a(u) + nh + 1] for u, sc in tiles}
    grow = {(u, sc): gam_r[u[0]][col_a(u):col_a(u) + 1, sc * sc_rows:(sc + 1) * sc_rows]
            for u, sc in tiles}
    decay = {t: jnp.exp(jnp.where(incl, gcol[t] - grow[t], -jnp.inf)) for t in tiles}
    eg = {t: jnp.exp(gcol[t]) for t in tiles}
    kb = {t: k_t[t].astype(BF16) for t in tiles}
    qb = {t: q_t[t].astype(BF16) for t in tiles}
    kk = {t: _dot_nt(kb[t], kb[t]) for t in tiles}
    qk = {t: (_dot_nt(qb[t], kb[t]) * decay[t]).astype(BF16) for t in tiles}
    neg_l = {t: jnp.where(strict, -(bcol[t] * kk[t] * decay[t]), 0.0) for t in tiles}
    x1 = {t: jnp.where(m16, neg_l[t], 0.0).astype(BF16) for t in tiles}
    p = {t: eye + x1[t].astype(F32) for t in tiles}
    xp = x1
    for _ in range(3):
        xp = {t: _dot(xp[t], xp[t]).astype(BF16) for t in tiles}
        p = {t: p[t] + _dot(p[t].astype(BF16), xp[t]) for t in tiles}
    for mask in (m32, m64):
        pb = {t: p[t].astype(BF16) for t in tiles}
        pe = {t: _dot(pb[t], jnp.where(mask, neg_l[t], 0.0).astype(BF16)).astype(BF16) for t in tiles}
        p = {t: p[t] + _dot(pe[t], pb[t]) for t in tiles}
    rhs = {t: jnp.concatenate([bcol[t] * v_t[t], (bcol[t] * eg[t]) * k_t[t]], axis=1).astype(BF16)
           for t in tiles}
    sol = {t: _dot(p[t].astype(BF16), rhs[t]) for t in tiles}
    kd = {t: (k_t[t] * jnp.exp(tcol[t] - gcol[t])).astype(BF16) for t in tiles}
    wkqg = {t: jnp.concatenate([sol[t][:, hd:], q_t[t] * eg[t]], axis=0).astype(BF16) for t in tiles}

    s = {u: s_ref[u[0], u[1]] for u in heads}
    for sc in sc_order:
        w_parts = {u: [None, None] for u in heads}
        og_parts = {u: [None, None] for u in heads}
        for ch in ch_order:
            c0 = ch * GDN_CHUNK
            for h in heads:
                t = (h, sc)
                lhs = jnp.concatenate([wkqg[t][c0:c0 + GDN_CHUNK],
                                       wkqg[t][sc_rows + c0:sc_rows + c0 + GDN_CHUNK]], axis=0)
                prod = _dot(lhs, s[h].astype(BF16))
                w_ch = sol[t][c0:c0 + GDN_CHUNK, :hd] - prod[:GDN_CHUNK]
                og_parts[h][ch] = prod[GDN_CHUNK:]
                w_parts[h][ch] = w_ch
                gc = jnp.exp(tcol[t][c0:c0 + 1, :])
                s[h] = gc * s[h] + _dot_tn(kd[t][c0:c0 + GDN_CHUNK], w_ch.astype(BF16))
        for h in heads:
            w_full = jnp.concatenate(w_parts[h], axis=0).astype(BF16)
            o = jnp.concatenate(og_parts[h], axis=0) + _dot(qk[(h, sc)], w_full)
            o_ref[h[0], sc * sc_rows:(sc + 1) * sc_rows, h[1] * hd:(h[1] + 1) * hd] = o.astype(o_ref.dtype)
    for h in heads:
        s_ref[h[0], h[1]] = s[h]


def _gdn(qkv, gates, gates_t, conv_w, gp, gpt, n_ctx_blk, reverse):
    bsz, s, w3 = qkv.shape
    nb = s // TOK_BLK
    halo_per_blk = TOK_BLK // HALO_ROWS
    n_halo = s // HALO_ROWS

    def blk(i):
        return _gdn_block(i, nb, n_ctx_blk, reverse)

    grp = GDN_BATCH_GROUP if bsz % GDN_BATCH_GROUP == 0 else 1
    tok_qkv = pl.BlockSpec((grp, TOK_BLK, w3), lambda b, i: (b, blk(i), 0))
    tok_o = pl.BlockSpec((grp, TOK_BLK, GDN_WIDTH), lambda b, i: (b, blk(i), 0))
    gate_specs = [pl.BlockSpec((grp, TOK_BLK, LANES), lambda b, i: (b, blk(i), 0)),
                  pl.BlockSpec((grp, 16, TOK_BLK), lambda b, i: (b, 0, blk(i)))]
    par_specs = [pl.BlockSpec(gp.shape, lambda b, i: (0, 0)),
                 pl.BlockSpec(gpt.shape, lambda b, i: (0, 0, 0))]
    state = pltpu.VMEM((grp, GDN_HEADS, GDN_HEAD_DIM, GDN_HEAD_DIM), F32)
    o_shape = jax.ShapeDtypeStruct((bsz, s, GDN_WIDTH), BF16)
    body = functools.partial(_gdn_kernel, reverse=reverse, n_blk=nb, n_ctx_blk=n_ctx_blk)
    if reverse:
        return pl.pallas_call(
            body, grid=(bsz // grp, nb),
            in_specs=[tok_qkv] + gate_specs + par_specs,
            out_specs=tok_o, out_shape=o_shape, scratch_shapes=[state],
            compiler_params=_cparams(("parallel", "arbitrary")), name="gdn_bwd",
        )(qkv, gates, gates_t, gp, gpt)
    halo_specs = [
        pl.BlockSpec((grp, HALO_ROWS, w3),
                     lambda b, i: (b, jnp.maximum(blk(i) * halo_per_blk - 1, 0), 0)),
        pl.BlockSpec((grp, HALO_ROWS, w3),
                     lambda b, i: (b, jnp.minimum((blk(i) + 1) * halo_per_blk, n_halo - 1), 0))]
    return pl.pallas_call(
        body, grid=(bsz // grp, nb),
        in_specs=[tok_qkv] + halo_specs + gate_specs + [pl.BlockSpec(conv_w.shape, lambda b, i: (0, 0))]
        + par_specs,
        out_specs=[tok_o, tok_qkv],
        out_shape=[o_shape, jax.ShapeDtypeStruct((bsz, s, w3), BF16)],
        scratch_shapes=[pltpu.VMEM((grp, TOK_BLK + 16, w3), F32), state],
        compiler_params=_cparams(("parallel", "arbitrary")), name="gdn_fwd",
    )(qkv, qkv, qkv, gates, gates_t, conv_w, gp, gpt)


def _merge_kernel(*refs, alpha, n_experts):
    (x_ref, ug_ref, vg_ref, of_ref, ob_ref, zs_ref, sga_ref, sgb_ref, mods_ref,
     slg_ref, slb_ref, sw_ref, sbias_ref, nw_ref, wpa_ref, wpb_ref, wo_ref,
     l1g_ref, l1b_ref) = refs[:19]
    if n_experts:
        router_ref, x1_ref, h2_ref, rcol_ref, rrow_ref, cnt_ref = refs[19:]
    else:
        x1_ref, h2_ref = refs[19:]
    tb = TOK_BLK
    vn = (_norm(vg_ref[0].astype(F32)) * slg_ref[...] + slb_ref[...]).astype(BF16)
    gd = vn.shape[1] // SGU_GROUPS
    rows = []
    for c in range(tb // SGU_CHUNK):
        cols = []
        for g in range(SGU_GROUPS):
            cols.append(_dot(sw_ref[g], vn[c * SGU_CHUNK:(c + 1) * SGU_CHUNK, g * gd:(g + 1) * gd]))
        rows.append(jnp.concatenate(cols, axis=1) + sbias_ref[...])
    mixed = jnp.concatenate(rows, axis=0)
    oa = (ug_ref[0].astype(F32) * mixed).astype(BF16)
    osum = of_ref[0].astype(F32) + ob_ref[0].astype(F32)
    heads = []
    for h in range(GDN_HEADS):
        oh = osum[:, h * GDN_HEAD_DIM:(h + 1) * GDN_HEAD_DIM]
        heads.append(oh * lax.rsqrt(jnp.mean(oh * oh, axis=-1, keepdims=True) + LN_EPS) * nw_ref[...])
    ogd = (jnp.concatenate(heads, axis=1) * zs_ref[0].astype(F32)).astype(BF16)
    y = (sga_ref[0].astype(F32) * _dot(oa, wpa_ref[...])
         + sgb_ref[0].astype(F32) * _dot(ogd, wpb_ref[...]))
    t = _dot(y.astype(BF16), wo_ref[...])
    x1 = _norm(alpha * x_ref[0] + mods_ref[0, 0, 2:3, :] * t) * l1g_ref[...] + l1b_ref[...]
    x1_ref[0] = x1
    h2 = _norm(x1) * (1.0 + mods_ref[0, 0, 4:5, :]) + mods_ref[0, 0, 3:4, :]
    h2_ref[0] = h2.astype(BF16)
    if n_experts:
        logits = _dot3(h2, router_ref[...])
        lane = lax.broadcasted_iota(jnp.int32, logits.shape, 1).astype(F32)
        lg = jnp.where(lane < n_experts, logits, -jnp.inf)
        m1 = jnp.max(lg, axis=-1, keepdims=True)
        i1 = jnp.min(jnp.where(lg == m1, lane, float(LANES)), axis=-1, keepdims=True)
        lg2 = jnp.where(lane == i1, -jnp.inf, lg)
        m2 = jnp.max(lg2, axis=-1, keepdims=True)
        i2 = jnp.min(jnp.where(lg2 == m2, lane, float(LANES)), axis=-1, keepdims=True)
        e2 = jnp.exp(m2 - m1)
        den = 1.0 + e2
        gate = jnp.where(lane == i1, 1.0 / den, 0.0) + jnp.where(lane == i2, e2 / den, 0.0)
        sel = jnp.where(jnp.logical_or(lane == i1, lane == i2), 1.0, 0.0)
        selb = sel.astype(BF16)
        ti = lax.broadcasted_iota(jnp.int32, (tb, tb), 0)
        tj = lax.broadcasted_iota(jnp.int32, (tb, tb), 1)
        before = jnp.where(tj < ti, 1.0, 0.0).astype(BF16)
        rcol_ref[0] = jnp.where(sel > 0.0, _dot(before, selb), MOE_NOT_SELECTED)
        er = lax.broadcasted_iota(jnp.int32, (8, LANES), 0)
        ec = lax.broadcasted_iota(jnp.int32, (8, LANES), 1)
        eye8 = jnp.where(er == ec, 1.0, 0.0).astype(BF16)
        sel_t = _dot_nt(eye8, selb)
        after = jnp.where(ti < tj, 1.0, 0.0).astype(BF16)
        rrow_ref[0, 0, 0:8, :] = jnp.where(sel_t > 0.0, _dot(sel_t.astype(BF16), after), MOE_NOT_SELECTED)
        g_hi = gate.astype(BF16)
        g_mid = (gate - g_hi.astype(F32)).astype(BF16)
        g_lo = (gate - g_hi.astype(F32) - g_mid.astype(F32)).astype(BF16)
        rrow_ref[0, 0, 8:16, :] = _dot_nt(eye8, g_hi) + (_dot_nt(eye8, g_mid) + _dot_nt(eye8, g_lo))
        cnt_ref[0, 0] = _dot(jnp.ones((8, tb), BF16), selb)


def _merge(x, ug, vg, o_f, o_b, zs, sga, sgb, mods, sgu_ln_g, sgu_ln_b, sgu_w, sgu_bias, norm_w,
           w_pa, w_pb, w_o, ln_g, ln_b, n_ctx_blk, alpha, router=None):
    bsz, s, d = x.shape
    nb = s // TOK_BLK
    sw = d // 2
    n_experts = 0 if router is None else router[1]

    def tok(width):
        return pl.BlockSpec((1, TOK_BLK, width), lambda b, i: (b, i, 0))

    def full(a):
        return pl.BlockSpec(a.shape, lambda b, i: (0,) * a.ndim)

    consts = [sgu_ln_g.reshape(1, sw), sgu_ln_b.reshape(1, sw), sgu_w, sgu_bias,
              norm_w.reshape(1, GDN_HEAD_DIM), w_pa, w_pb, w_o, ln_g.reshape(1, d), ln_b.reshape(1, d)]
    if n_experts:
        consts.append(router[0])
    in_specs = [tok(d), tok(sw), tok(sw), tok(GDN_WIDTH), tok(GDN_WIDTH), tok(GDN_WIDTH), tok(d), tok(d),
                pl.BlockSpec((1, 1, 6, d), lambda b, i: (b, jnp.where(i < n_ctx_blk, 0, 1), 0, 0))]
    in_specs += [full(a) for a in consts]
    out_specs = [tok(d), tok(d)]
    out_shape = [jax.ShapeDtypeStruct((bsz, s, d), F32), jax.ShapeDtypeStruct((bsz, s, d), BF16)]
    if n_experts:
        assert n_experts <= 8
        out_specs += [tok(LANES),
                      pl.BlockSpec((1, 1, 16, TOK_BLK), lambda b, i: (b, i, 0, 0)),
                      pl.BlockSpec((1, 1, 8, LANES), lambda b, i: (b, i, 0, 0))]
        out_shape += [jax.ShapeDtypeStruct((bsz, s, LANES), F32),
                      jax.ShapeDtypeStruct((bsz, nb, 16, TOK_BLK), F32),
                      jax.ShapeDtypeStruct((bsz, nb, 8, LANES), F32)]
    return pl.pallas_call(
        functools.partial(_merge_kernel, alpha=alpha, n_experts=n_experts),
        grid=(bsz, nb),
        in_specs=in_specs,
        out_specs=out_specs,
        out_shape=out_shape,
        compiler_params=_cparams(("parallel", "arbitrary")),
        name="merge",
    )(x, ug, vg, o_f, o_b, zs, sga, sgb, mods, *consts)


def _ffn_kernel(h_ref, w1_ref, w3_ref, w2_ref, o_ref, acc_ref):
    f = pl.program_id(1)

    @pl.when(f == 0)
    def _():
        acc_ref[...] = jnp.zeros_like(acc_ref)

    h = h_ref[...]
    mid = (_silu(_dot(h, w1_ref[...].astype(BF16))) * _dot(h, w3_ref[...].astype(BF16))).astype(BF16)
    acc_ref[...] += _dot(mid, w2_ref[...].astype(BF16))

    @pl.when(f == pl.num_programs(1) - 1)
    def _():
        o_ref[...] = acc_ref[...].astype(o_ref.dtype)


def _ffn(h, w1, w3, w2, tm, tf):
    t, d = h.shape
    ff = w1.shape[1]
    return pl.pallas_call(
        _ffn_kernel,
        grid=(t // tm, ff // tf),
        in_specs=[pl.BlockSpec((tm, d), lambda i, f: (i, 0)),
                  pl.BlockSpec((d, tf), lambda i, f: (0, f)),
                  pl.BlockSpec((d, tf), lambda i, f: (0, f)),
                  pl.BlockSpec((tf, d), lambda i, f: (f, 0))],
        out_specs=pl.BlockSpec((tm, d), lambda i, f: (i, 0)),
        out_shape=jax.ShapeDtypeStruct((t, d), BF16),
        scratch_shapes=[pltpu.VMEM((tm, d), F32)],
        compiler_params=_cparams(("parallel", "arbitrary")),
        name="ffn",
    )(h, w1, w3, w2)


MOE_GATHER_ROWS = 128
MOE_FFN_ROWS = 512
MOE_ROW_ALIGN = 16
MOE_NOT_SELECTED = -4096.0
MOE_TOKEN_TILE = 3072
FFN_TOKEN_TILE = 1536


def _moe_rows_capacity(tm):
    rows = tm + TOK_BLK + LANES
    return -(-rows // MOE_FFN_ROWS) * MOE_FFN_ROWS


def _moe_kernel(cnt_ref, off_ref, tot_ref, h_ref, rcol_ref, rrow_ref, w1_ref, w3_ref, w2_ref,
                o_ref, xs_ref, ys_ref, wrow_ref, *, n_experts):
    i = pl.program_id(0)
    e = pl.program_id(1)
    f = pl.program_id(2)
    tm = h_ref.shape[0]
    n_sub = tm // TOK_BLK
    gr = MOE_GATHER_ROWS
    fr = MOE_FFN_ROWS
    tot = tot_ref[i * n_experts + e]
    quarter = fr // 4
    n_quarters = _blk_id(tot + (quarter - 1), quarter)
    n_full = _blk_id(n_quarters, 4)
    has_half = jnp.bitwise_and(n_quarters, 2) != 0
    has_quarter = jnp.bitwise_and(n_quarters, 1) != 0

    def piece(j):
        base = (i * n_sub + j) * n_experts + e
        off = off_ref[base]
        start = pl.multiple_of(_blk_id(off, MOE_ROW_ALIGN) * MOE_ROW_ALIGN, MOE_ROW_ALIGN)
        return start, off - start, cnt_ref[base]

    @pl.when(jnp.logical_and(e == 0, f == 0))
    def _():
        o_ref[...] = jnp.zeros_like(o_ref)
        xs_ref[...] = jnp.zeros_like(xs_ref)
        ys_ref[...] = jnp.zeros_like(ys_ref)
        wrow_ref[...] = jnp.zeros_like(wrow_ref)

    @pl.when(f == 0)
    def _():
        def clear(rb, carry):
            rows = pl.ds(pl.multiple_of(rb * fr, fr), fr)
            xs_ref[rows, :] = jnp.zeros((fr, xs_ref.shape[1]), BF16)
            ys_ref[rows, :] = jnp.zeros((fr, ys_ref.shape[1]), F32)
            wrow_ref[rows, :] = jnp.zeros((fr, LANES), F32)
            return carry

        lax.fori_loop(0, _blk_id(tot + (fr - 1), fr) + 1, clear, 0)

        row_id = lax.broadcasted_iota(jnp.int32, (gr, TOK_BLK), 0).astype(F32)
        for j in range(n_sub):
            start, shift, cnt = piece(j)
            rank = rrow_ref[j, pl.ds(e, 1), :]
            gate = rrow_ref[j, pl.ds(8 + e, 1), :]
            row0 = row_id - shift.astype(F32)

            def gather(b, carry, j=j, start=start, rank=rank, gate=gate, row0=row0):
                hit = rank == row0 + (b * gr).astype(F32)
                rows = _dot(jnp.where(hit, 1.0, 0.0).astype(BF16), h_ref[j * TOK_BLK:(j + 1) * TOK_BLK, :])
                dst = pl.ds(pl.multiple_of(start + b * gr, MOE_ROW_ALIGN), gr)
                xs_ref[dst, :] = xs_ref[dst, :] + rows.astype(BF16)
                wgt = jnp.sum(jnp.where(hit, gate, 0.0), axis=-1, keepdims=True)
                wrow_ref[dst, :] = wrow_ref[dst, :] + jnp.broadcast_to(wgt, (gr, LANES))
                return carry

            lax.fori_loop(0, _blk_id(shift + cnt + (gr - 1), gr), gather, 0)

    def ffn_rows(r0, n):
        x = xs_ref[pl.ds(r0, n), :]
        mid = _silu(_dot(x, w1_ref[0])) * _dot(x, w3_ref[0]) * wrow_ref[pl.ds(r0, n), :][:, 0:1]
        ys_ref[pl.ds(r0, n), :] += _dot(mid.astype(BF16), w2_ref[0].astype(BF16))

    def ffn(rb, carry):
        ffn_rows(pl.multiple_of(rb * fr, fr), fr)
        return carry

    lax.fori_loop(0, n_full, ffn, 0)

    @pl.when(has_half)
    def _():
        ffn_rows(pl.multiple_of(n_full * fr, fr), 2 * quarter)

    @pl.when(has_quarter)
    def _():
        r0 = n_full * fr + jnp.where(has_half, 2 * quarter, 0)
        ffn_rows(pl.multiple_of(r0, quarter), quarter)

    @pl.when(f == pl.num_programs(2) - 1)
    def _():
        col_id = lax.broadcasted_iota(jnp.int32, (TOK_BLK, LANES), 1).astype(F32)
        lane = lax.broadcasted_iota(jnp.int32, (TOK_BLK, LANES), 1)
        for j in range(n_sub):
            start, shift, cnt = piece(j)
            tok = slice(j * TOK_BLK, (j + 1) * TOK_BLK)

            def rank_col(tok=tok):
                return jnp.sum(jnp.where(lane == e, rcol_ref[tok, :], 0.0), axis=-1, keepdims=True)

            for w in range(TOK_BLK // LANES + 1):

                @pl.when(jnp.logical_and(cnt > 0, shift + cnt > w * LANES))
                def _(w=w, tok=tok, start=start, shift=shift, rank_col=rank_col):
                    cols = col_id - (shift - w * LANES).astype(F32)
                    onehot = jnp.where(rank_col() == cols, 1.0, 0.0).astype(BF16)
                    rows = pl.ds(pl.multiple_of(start + w * LANES, MOE_ROW_ALIGN), LANES)
                    o_ref[tok, :] += _dot(onehot, ys_ref[rows, :].astype(BF16))


def _moe(h, rcol, rrow, cnt, off, tot, w1, w3, w2, tm, tf):
    t, d = h.shape
    ne, _, ff = w1.shape
    n_sub = tm // TOK_BLK
    cap = _moe_rows_capacity(tm)
    once = pl.Buffered(1)
    grid_spec = pltpu.PrefetchScalarGridSpec(
        num_scalar_prefetch=3,
        grid=(t // tm, ne, ff // tf),
        in_specs=[pl.BlockSpec((tm, d), lambda i, e, f, *_: (i, 0), pipeline_mode=once),
                  pl.BlockSpec((tm, LANES), lambda i, e, f, *_: (i, 0), pipeline_mode=once),
                  pl.BlockSpec((n_sub, 16, TOK_BLK), lambda i, e, f, *_: (i, 0, 0)),
                  pl.BlockSpec((1, d, tf), lambda i, e, f, *_: (e, 0, f)),
                  pl.BlockSpec((1, d, tf), lambda i, e, f, *_: (e, 0, f)),
                  pl.BlockSpec((1, tf, d), lambda i, e, f, *_: (e, f, 0))],
        out_specs=pl.BlockSpec((tm, d), lambda i, e, f, *_: (i, 0), pipeline_mode=once),
        scratch_shapes=[pltpu.VMEM((cap, d), BF16), pltpu.VMEM((cap, d), F32),
                        pltpu.VMEM((cap, LANES), F32)],
    )
    return pl.pallas_call(
        functools.partial(_moe_kernel, n_experts=ne),
        grid_spec=grid_spec,
        out_shape=jax.ShapeDtypeStruct((t, d), F32),
        compiler_params=_cparams(("parallel", "arbitrary", "arbitrary")),
        name="moe",
    )(cnt, off, tot, h, rcol, rrow, w1, w3, w2)


def _final_kernel(x_ref, f_ref, mods_ref, g_ref, b_ref, o_ref, *, alpha):
    x = alpha * x_ref[0] + mods_ref[0, 0, 5:6, :] * f_ref[0].astype(F32)
    o_ref[0] = _norm(x) * g_ref[...] + b_ref[...]


def _final(x1, f, mods, ln_g, ln_b, n_ctx_blk, alpha):
    bsz, s, d = x1.shape
    nb = s // TOK_BLK - n_ctx_blk
    return pl.pallas_call(
        functools.partial(_final_kernel, alpha=alpha),
        grid=(bsz, nb),
        in_specs=[pl.BlockSpec((1, TOK_BLK, d), lambda b, i: (b, i + n_ctx_blk, 0)),
                  pl.BlockSpec((1, TOK_BLK, d), lambda b, i: (b, i + n_ctx_blk, 0)),
                  pl.BlockSpec((1, 1, 6, d), lambda b, i: (b, 1, 0, 0)),
                  pl.BlockSpec((1, d), lambda b, i: (0, 0)),
                  pl.BlockSpec((1, d), lambda b, i: (0, 0))],
        out_specs=pl.BlockSpec((1, TOK_BLK, d), lambda b, i: (b, i, 0)),
        out_shape=jax.ShapeDtypeStruct((bsz, nb * TOK_BLK, d), F32),
        compiler_params=_cparams(("parallel", "arbitrary")),
        name="final",
    )(x1, f, mods, ln_g.reshape(1, d), ln_b.reshape(1, d))


def _token_tile(t, target):
    best = TOK_BLK
    for m in range(1, t // TOK_BLK + 1):
        cand = m * TOK_BLK
        if cand <= target and t % cand == 0:
            best = cand
    return best


def _ff_tile(ff, target):
    best = LANES
    for m in range(1, ff // LANES + 1):
        cand = m * LANES
        if cand <= target and ff % cand == 0:
            best = cand
    return best


def kernel(x, c, ctx, c_ctx, w_ada, b_ada, w_in, sgu_ln_g, sgu_ln_b, sgu_w, sgu_b, conv_w, a_log, dt_bias, gdn_norm_w, w_pa, w_pb, w_o, ln1_g, ln1_b, ln2_g, ln2_b, ffn_w1, ffn_w3, ffn_w2, moe_router, moe_w1, moe_w3, moe_w2):
    bsz, seq, d = x.shape
    ctx_len = ctx.shape[1]
    depth = w_ada.shape[0]
    alpha = (2.0 * depth) ** 0.25
    s = ctx_len + seq
    assert ctx_len % TOK_BLK == 0 and seq % TOK_BLK == 0
    assert d // 2 == SGU_GROUPS * LANES and bsz < 8
    n_ctx_blk = ctx_len // TOK_BLK
    sw = d // 2
    gw = GDN_WIDTH
    nh = GDN_HEADS

    cond = jnp.zeros((8, d), F32).at[:bsz].set(c).at[bsz].set(c_ctx)

    t_all = bsz * s
    tm_ffn = _token_tile(t_all, FFN_TOKEN_TILE)

    f_prev = None
    prev = None
    x_cur = x
    for l in range(depth):
        mod = _ada(cond, w_ada[l], b_ada[l]).reshape(8, 6, d)
        mods = jnp.stack([jnp.broadcast_to(mod[bsz], (bsz, 6, d)), mod[:bsz]], axis=1)

        g0 = 2 * sw + 4 * gw
        w_main = jnp.concatenate([w_in[l][:, :g0], w_in[l][:, g0 + 4 * nh:]], axis=1).astype(BF16)
        w_gate = jnp.pad(w_in[l][:, g0:g0 + 4 * nh], ((0, 0), (0, LANES - 4 * nh))).astype(BF16)
        w_gate_t = w_in[l][:, g0:g0 + 4 * nh].T.astype(BF16)
        outs = _inproj(x_cur, w_main, w_gate, w_gate_t, mods, n_ctx_blk, prev=prev,
                       ctx=ctx if prev is None else None, alpha=alpha)
        x_cur, ug, vg, qkv, zs, sga, sgb, gates, gates_t = outs

        al = jnp.zeros((LANES,), F32)
        db = jnp.zeros((LANES,), F32)
        for dd in range(2):
            al = al.at[2 * nh * dd:2 * nh * dd + nh].set(a_log[l, dd])
            db = db.at[2 * nh * dd:2 * nh * dd + nh].set(dt_bias[l, dd])
        gp = jnp.zeros((8, LANES), F32).at[0].set(al).at[1].set(db)
        gpt = jnp.stack([jnp.broadcast_to(al[:16, None], (16, LANES)),
                         jnp.broadcast_to(db[:16, None], (16, LANES))])
        cw = jnp.pad(conv_w[l], ((0, 8 - CONV_K), (0, 0)))
        o_f, qkvn = _gdn(qkv, gates, gates_t, cw, gp, gpt, n_ctx_blk, reverse=False)
        o_b = _gdn(qkvn, gates, gates_t, None, gp, gpt, n_ctx_blk, reverse=True)

        sgu_bias = jnp.repeat(sgu_b[l].T, LANES, axis=1)
        is_moe = l % 2 == 1
        router = None
        if is_moe:
            r = moe_router[l // 2]
            router = (jnp.pad(r, ((0, 0), (0, LANES - r.shape[1]))), r.shape[1])
        outs = _merge(x_cur, ug, vg, o_f, o_b, zs, sga, sgb, mods, sgu_ln_g[l], sgu_ln_b[l],
                      sgu_w[l].astype(BF16), sgu_bias, gdn_norm_w[l], w_pa[l].astype(BF16),
                      w_pb[l].astype(BF16), w_o[l].astype(BF16), ln1_g[l], ln1_b[l], n_ctx_blk, alpha,
                      router=router)
        x1, h2 = outs[0], outs[1]

        h2f = h2.reshape(t_all, d)
        if is_moe:
            ne = moe_router.shape[-1]
            rcol = outs[2].reshape(t_all, LANES)
            rrow = outs[3].reshape(t_all // TOK_BLK, 16, TOK_BLK)
            tm = _token_tile(t_all, MOE_TOKEN_TILE)
            n_sub = tm // TOK_BLK
            cnt = outs[4][:, :, 0, :ne].astype(jnp.int32).reshape(t_all // tm, n_sub, ne)
            off = jnp.cumsum(cnt, axis=1) - cnt
            tot = jnp.sum(cnt, axis=1)
            tf = _ff_tile(moe_w1.shape[-1], 512)
            f = _moe(h2f, rcol, rrow, cnt.reshape(-1), off.reshape(-1), tot.reshape(-1),
                     moe_w1[l // 2].astype(BF16), moe_w3[l // 2].astype(BF16),
                     moe_w2[l // 2], tm, tf)
        else:
            tf = _ff_tile(ffn_w1.shape[-1], 512)
            f = _ffn(h2f, ffn_w1[l // 2], ffn_w3[l // 2], ffn_w2[l // 2], tm_ffn, tf)
        f = f.reshape(bsz, s, d)
        prev = (f, mods, ln2_g[l], ln2_b[l])
        f_prev, mods_prev = f, mods
        x_cur = x1

    return _final(x_cur, f_prev, mods_prev, ln2_g[depth - 1], ln2_b[depth - 1], n_ctx_blk, alpha)
```

```python
import functools
import math

import jax
import jax.numpy as jnp
from jax import lax
from jax.experimental import pallas as pl
from jax.experimental.pallas import tpu as pltpu

F32 = jnp.float32
BF16 = jnp.bfloat16

LN_EPS = 1e-6
SGU_CHUNK = 128
SGU_GROUPS = 4
GDN_HEADS = 4
GDN_HEAD_DIM = 128
GDN_WIDTH = GDN_HEADS * GDN_HEAD_DIM
GDN_CHUNK = 64
CONV_K = 5
TOP_K = 2

LANES = 128
TOK_BLK = 256
HALO_ROWS = 16
GDN_BATCH_GROUP = 4
MERGE_BATCH_GROUP = 2
VMEM_LIMIT = 56 * 1024 * 1024


def _cparams(sem):
    return pltpu.CompilerParams(dimension_semantics=sem, vmem_limit_bytes=VMEM_LIMIT)


def _dot(a, b):
    return jnp.dot(a, b, preferred_element_type=F32)


def _dot_nt(a, b):
    return lax.dot_general(a, b, (((1,), (1,)), ((), ())), preferred_element_type=F32)


def _dot_tn(a, b):
    return lax.dot_general(a, b, (((0,), (0,)), ((), ())), preferred_element_type=F32)


def _split(a):
    hi = a.astype(BF16)
    lo = (a - hi.astype(F32)).astype(BF16)
    return hi, lo


def _dot3(a, b):
    ah, al = _split(a)
    bh, bl = _split(b)
    return _dot(ah, bh) + (_dot(ah, bl) + _dot(al, bh))


def _dot2_exact_lhs(a_bf, b):
    bh, bl = _split(b)
    return _dot(a_bf, bh) + _dot(a_bf, bl)


def _dot2_exact_rhs(a, b_bf):
    ah, al = _split(a)
    return _dot(ah, b_bf) + _dot(al, b_bf)


def _norm(x):
    mu = jnp.mean(x, axis=-1, keepdims=True)
    xc = x - mu
    var = jnp.mean(xc * xc, axis=-1, keepdims=True)
    return xc * lax.rsqrt(var + LN_EPS)


def _sigmoid(x):
    return 1.0 / (1.0 + jnp.exp(-x))


def _silu(x):
    return x * _sigmoid(x)


def _gelu(x):
    return 0.5 * x * (1.0 + lax.erf(x * (2.0 ** -0.5)))


def _blk_id(idx, size):
    return jnp.right_shift(idx, int(math.log2(size)))


def _softplus(x):
    return jnp.maximum(x, 0.0) + jnp.log1p(jnp.exp(-jnp.abs(x)))


def _ada_kernel(c_ref, w_ref, b_ref, o_ref):
    o_ref[...] = _dot3(_silu(c_ref[...]), w_ref[...]) + b_ref[...]


def _ada(cond, w, b):
    rows, d = cond.shape
    n = w.shape[1]
    tn = n // 6
    return pl.pallas_call(
        _ada_kernel,
        grid=(n // tn,),
        in_specs=[pl.BlockSpec((rows, d), lambda j: (0, 0)),
                  pl.BlockSpec((d, tn), lambda j: (0, j)),
                  pl.BlockSpec((1, tn), lambda j: (0, j))],
        out_specs=pl.BlockSpec((rows, tn), lambda j: (0, j)),
        out_shape=jax.ShapeDtypeStruct((rows, n), F32),
        compiler_params=_cparams(("arbitrary",)),
        name="ada",
    )(cond, w, b.reshape(1, n))


def _inproj_kernel(*refs, has_prev, alpha, seg, n_ctx_blk):
    if has_prev:
        (x_ref, f_ref, mprev_ref, lng_ref, lnb_ref, mods_ref, w_ref, wg_ref, wgt_ref,
         xo_ref, ug_ref, vg_ref, qkv_ref, zs_ref, sga_ref, sgb_ref, gates_ref, gates_t_ref) = refs
        x = alpha * x_ref[0] + mprev_ref[0, 0, 5:6, :] * f_ref[0].astype(F32)
        x = _norm(x) * lng_ref[...] + lnb_ref[...]
        xo_ref[0] = x
    else:
        (c_ref, x_ref, mods_ref, w_ref, wg_ref, wgt_ref,
         xo_ref, ug_ref, vg_ref, qkv_ref, zs_ref, sga_ref, sgb_ref, gates_ref, gates_t_ref) = refs
        x = jnp.where(pl.program_id(1) < n_ctx_blk, c_ref[0], x_ref[0])
        xo_ref[0] = x
    h = _norm(x) * (1.0 + mods_ref[0, 0, 1:2, :]) + mods_ref[0, 0, 0:1, :]
    hb = h.astype(BF16)

    def proj(lo, hi):
        return _dot(hb, w_ref[:, lo:hi])

    sw, gw = seg
    c0 = 0
    ug_ref[0] = _gelu(proj(c0, c0 + sw)).astype(BF16)
    c0 += sw
    vg_ref[0] = _gelu(proj(c0, c0 + sw)).astype(BF16)
    c0 += sw
    for j in range(3):
        qkv_ref[0, :, j * gw:(j + 1) * gw] = proj(c0, c0 + gw).astype(BF16)
        c0 += gw
    zs_ref[0] = _silu(proj(c0, c0 + gw)).astype(BF16)
    c0 += gw
    d = sga_ref.shape[-1]
    half = d // 2
    for j in range(2):
        sga_ref[0, :, j * half:(j + 1) * half] = _sigmoid(proj(c0, c0 + half)).astype(BF16)
        c0 += half
    for j in range(2):
        sgb_ref[0, :, j * half:(j + 1) * half] = _sigmoid(proj(c0, c0 + half)).astype(BF16)
        c0 += half
    gates_ref[0] = _dot(hb, wg_ref[...])
    gates_t_ref[0] = _dot_nt(wgt_ref[...], hb)


def _inproj(x, w_main, w_gate, w_gate_t, mods, n_ctx_blk, prev=None, ctx=None, alpha=1.0):
    bsz, s, d = x.shape
    if ctx is not None:
        s += ctx.shape[1]
    nb = s // TOK_BLK
    sw = d // 2
    gw = GDN_WIDTH
    n_main = w_main.shape[1]

    def tok(width):
        return pl.BlockSpec((1, TOK_BLK, width), lambda b, i: (b, i, 0))

    def mod_spec():
        return pl.BlockSpec((1, 1, 6, d), lambda b, i: (b, jnp.where(i < n_ctx_blk, 0, 1), 0, 0))

    def full(shape):
        return pl.BlockSpec(shape, lambda b, i: (0,) * len(shape))

    if ctx is not None:
        in_specs = [pl.BlockSpec((1, TOK_BLK, d), lambda b, i: (b, jnp.minimum(i, n_ctx_blk - 1), 0)),
                    pl.BlockSpec((1, TOK_BLK, d), lambda b, i: (b, jnp.maximum(i - n_ctx_blk, 0), 0))]
        args = [ctx, x]
    else:
        in_specs = [tok(d)]
        args = [x]
    if prev is not None:
        f_prev, mods_prev, ln_g, ln_b = prev
        in_specs += [tok(d), mod_spec(), full((1, d)), full((1, d))]
        args += [f_prev, mods_prev, ln_g.reshape(1, d), ln_b.reshape(1, d)]
    in_specs += [mod_spec(), full((d, n_main)), full((d, LANES)), full(w_gate_t.shape)]
    args += [mods, w_main, w_gate, w_gate_t]

    out_specs = []
    out_shape = []
    out_specs.append(tok(d))
    out_shape.append(jax.ShapeDtypeStruct((bsz, s, d), F32))
    for width, dt in ((sw, BF16), (sw, BF16), (3 * gw, BF16), (gw, BF16), (d, BF16), (d, BF16),
                      (LANES, F32)):
        out_specs.append(tok(width))
        out_shape.append(jax.ShapeDtypeStruct((bsz, s, width), dt))
    out_specs.append(pl.BlockSpec((1, w_gate_t.shape[0], TOK_BLK), lambda b, i: (b, 0, i)))
    out_shape.append(jax.ShapeDtypeStruct((bsz, w_gate_t.shape[0], s), F32))

    return pl.pallas_call(
        functools.partial(_inproj_kernel, has_prev=prev is not None, alpha=alpha, seg=(sw, gw),
                          n_ctx_blk=n_ctx_blk),
        grid=(bsz, nb),
        in_specs=in_specs,
        out_specs=out_specs,
        out_shape=out_shape,
        compiler_params=_cparams(("parallel", "arbitrary")),
        name="inproj",
    )(*args)


def _gdn_block(i, n_blk, n_ctx_blk, reverse):
    if not reverse:
        return i
    return jnp.where(i < n_ctx_blk, n_ctx_blk - 1 - i, n_blk - 1 - (i - n_ctx_blk))


def _gdn_kernel(*refs, reverse, n_blk, n_ctx_blk):
    if reverse:
        qkvn_ref, g_ref, gt_ref, gp_ref, gpt_ref, o_ref, s_ref = refs
    else:
        (qkv_ref, prev_ref, next_ref, g_ref, gt_ref, cw_ref, gp_ref, gpt_ref,
         o_ref, qkvn_ref, win_ref, s_ref) = refs
    i = pl.program_id(1)
    blk = _gdn_block(i, n_blk, n_ctx_blk, reverse)
    tb = TOK_BLK
    hd = GDN_HEAD_DIM
    nh = GDN_HEADS
    d_idx = 1 if reverse else 0
    group = range(o_ref.shape[0])

    @pl.when(i == 0)
    def _():
        s_ref[...] = jnp.zeros_like(s_ref)

    if not reverse:
        first = jnp.logical_or(blk == 0, blk == n_ctx_blk)
        last = jnp.logical_or(blk == n_ctx_blk - 1, blk == n_blk - 1)
        for bb in group:
            prev = prev_ref[bb].astype(F32)[HALO_ROWS - 8:, :]
            nxt = next_ref[bb].astype(F32)[:8, :]
            win_ref[bb, 0:8, :] = jnp.where(first, 0.0, prev)
            win_ref[bb, 8:8 + tb, :] = qkv_ref[bb].astype(F32)
            win_ref[bb, 8 + tb:16 + tb, :] = jnp.where(last, 0.0, nxt)

    def conv_slab(bb, col):
        win = win_ref[bb, :, col:col + hd]
        n = win.shape[0]
        acc = None
        for j in range(CONV_K):
            term = cw_ref[j:j + 1, col:col + hd] * win
            shift = (CONV_K // 2 - j) % n
            if shift:
                term = pltpu.roll(term, shift, axis=0)
            acc = term if acc is None else acc + term
        return _silu(acc[8:8 + tb])

    def l2n(x):
        return x * lax.rsqrt(jnp.sum(x * x, axis=-1, keepdims=True) + LN_EPS)

    def qkv_slab(bb, col, kind):
        if reverse:
            return qkvn_ref[bb, :, col:col + hd].astype(F32)
        x = conv_slab(bb, col)
        if kind == 0:
            x = l2n(x) * (hd ** -0.5)
        elif kind == 1:
            x = l2n(x)
        qkvn_ref[bb, :, col:col + hd] = x.astype(BF16)
        return x

    lg_c = [-jnp.exp(gp_ref[0:1, :]) * _softplus(g_ref[bb] + gp_ref[1:2, :]) for bb in group]
    beta_c = [_sigmoid(g_ref[bb]) for bb in group]
    lg_r = [-jnp.exp(gpt_ref[0][:, 0:1]) * _softplus(gt_ref[bb] + gpt_ref[1][:, 0:1])
            for bb in group]

    ii = lax.broadcasted_iota(jnp.int32, (tb, tb), 0)
    jj = lax.broadcasted_iota(jnp.int32, (tb, tb), 1)
    same_chunk = _blk_id(ii, GDN_CHUNK) == _blk_id(jj, GDN_CHUNK)
    cum = jnp.logical_and(same_chunk, (ii <= jj) if reverse else (ii >= jj))
    cum_c = jnp.where(cum, 1.0, 0.0).astype(BF16)
    cum_r = jnp.where(jnp.logical_and(same_chunk, (jj <= ii) if reverse else (jj >= ii)),
                      1.0, 0.0).astype(BF16)
    ones_c = jnp.where(same_chunk, 1.0, 0.0).astype(BF16)
    gam_c = [_dot2_exact_lhs(cum_c, lg_c[bb]) for bb in group]
    tot_c = [_dot2_exact_lhs(ones_c, lg_c[bb]) for bb in group]
    gam_r = [_dot2_exact_rhs(lg_r[bb], cum_r) for bb in group]

    sc_rows = 2 * GDN_CHUNK
    pi = lax.broadcasted_iota(jnp.int32, (sc_rows, sc_rows), 0)
    pj = lax.broadcasted_iota(jnp.int32, (sc_rows, sc_rows), 1)
    tri_incl = (pi <= pj) if reverse else (pi >= pj)
    tri_strict = (pi < pj) if reverse else (pi > pj)

    def same(level):
        return _blk_id(pi, level) == _blk_id(pj, level)

    incl = jnp.logical_and(same(GDN_CHUNK), tri_incl)
    strict = jnp.logical_and(same(GDN_CHUNK), tri_strict)
    m16 = jnp.logical_and(strict, same(16))
    m32 = jnp.logical_and(jnp.logical_and(strict, same(32)), jnp.logical_not(same(16)))
    m64 = jnp.logical_and(strict, jnp.logical_not(same(32)))
    eye = jnp.where(pi == pj, 1.0, 0.0)

    n_sc = tb // sc_rows
    sc_order = range(n_sc - 1, -1, -1) if reverse else range(n_sc)
    ch_order = (1, 0) if reverse else (0, 1)

    heads = [(bb, h) for bb in group for h in range(nh)]
    tiles = [(u, sc) for u in heads for sc in range(n_sc)]
    q_all = {u: qkv_slab(u[0], u[1] * hd, 0) for u in heads}
    k_all = {u: qkv_slab(u[0], GDN_WIDTH + u[1] * hd, 1) for u in heads}
    v_all = {u: qkv_slab(u[0], 2 * GDN_WIDTH + u[1] * hd, 2) for u in heads}

    def rows(a, sc):
        return a[sc * sc_rows:(sc + 1) * sc_rows]

    def col_a(u):
        return 2 * nh * d_idx + u[1]

    q_t = {t: rows(q_all[t[0]], t[1]) for t in tiles}
    k_t = {t: rows(k_all[t[0]], t[1]) for t in tiles}
    v_t = {t: rows(v_all[t[0]], t[1]) for t in tiles}
    gcol = {(u, sc): rows(gam_c[u[0]], sc)[:, col_a(u):col_a(u) + 1] for u, sc in tiles}
    tcol = {(u, sc): rows(tot_c[u[0]], sc)[:, col_a(u):col_a(u) + 1] for u, sc in tiles}
    bcol = {(u, sc): rows(beta_c[u[0]], sc)[:, col_a(u) + nh:col_a(u) + nh + 1] for u, sc in tiles}
    grow = {(u, sc): gam_r[u[0]][col_a(u):col_a(u) + 1, sc * sc_rows:(sc + 1) * sc_rows]
            for u, sc in tiles}
    decay = {t: jnp.exp(jnp.where(incl, gcol[t] - grow[t], -jnp.inf)) for t in tiles}
    eg = {t: jnp.exp(gcol[t]) for t in tiles}
    kb = {t: k_t[t].astype(BF16) for t in tiles}
    qb = {t: q_t[t].astype(BF16) for t in tiles}
    kk = {t: _dot_nt(kb[t], kb[t]) for t in tiles}
    qk = {t: (_dot_nt(qb[t], kb[t]) * decay[t]).astype(BF16) for t in tiles}
    neg_l = {t: jnp.where(strict, -(bcol[t] * kk[t] * decay[t]), 0.0) for t in tiles}
    x1 = {t: jnp.where(m16, neg_l[t], 0.0).astype(BF16) for t in tiles}
    p = {t: eye + x1[t].astype(F32) for t in tiles}
    xp = x1
    for _ in range(3):
        xp = {t: _dot(xp[t], xp[t]).astype(BF16) for t in tiles}
        p = {t: p[t] + _dot(p[t].astype(BF16), xp[t]) for t in tiles}
    for mask in (m32, m64):
        pb = {t: p[t].astype(BF16) for t in tiles}
        pe = {t: _dot(pb[t], jnp.where(mask, neg_l[t], 0.0).astype(BF16)).astype(BF16) for t in tiles}
        p = {t: p[t] + _dot(pe[t], pb[t]) for t in tiles}
    rhs = {t: jnp.concatenate([bcol[t] * v_t[t], (bcol[t] * eg[t]) * k_t[t]], axis=1).astype(BF16)
           for t in tiles}
    sol = {t: _dot(p[t].astype(BF16), rhs[t]) for t in tiles}
    kd = {t: (k_t[t] * jnp.exp(tcol[t] - gcol[t])).astype(BF16) for t in tiles}
    wkqg = {t: jnp.concatenate([sol[t][:, hd:], q_t[t] * eg[t]], axis=0).astype(BF16) for t in tiles}

    s = {u: s_ref[u[0], u[1]] for u in heads}
    for sc in sc_order:
        w_parts = {u: [None, None] for u in heads}
        og_parts = {u: [None, None] for u in heads}
        for ch in ch_order:
            c0 = ch * GDN_CHUNK
            for h in heads:
                t = (h, sc)
                lhs = jnp.concatenate([wkqg[t][c0:c0 + GDN_CHUNK],
                                       wkqg[t][sc_rows + c0:sc_rows + c0 + GDN_CHUNK]], axis=0)
                prod = _dot(lhs, s[h].astype(BF16))
                w_ch = sol[t][c0:c0 + GDN_CHUNK, :hd] - prod[:GDN_CHUNK]
                og_parts[h][ch] = prod[GDN_CHUNK:]
                w_parts[h][ch] = w_ch
                gc = jnp.exp(tcol[t][c0:c0 + 1, :])
                s[h] = gc * s[h] + _dot_tn(kd[t][c0:c0 + GDN_CHUNK], w_ch.astype(BF16))
        for h in heads:
            w_full = jnp.concatenate(w_parts[h], axis=0).astype(BF16)
            o = jnp.concatenate(og_parts[h], axis=0) + _dot(qk[(h, sc)], w_full)
            o_ref[h[0], sc * sc_rows:(sc + 1) * sc_rows, h[1] * hd:(h[1] + 1) * hd] = o.astype(o_ref.dtype)
    for h in heads:
        s_ref[h[0], h[1]] = s[h]


def _gdn(qkv, gates, gates_t, conv_w, gp, gpt, n_ctx_blk, reverse):
    bsz, s, w3 = qkv.shape
    nb = s // TOK_BLK
    halo_per_blk = TOK_BLK // HALO_ROWS
    n_halo = s // HALO_ROWS

    def blk(i):
        return _gdn_block(i, nb, n_ctx_blk, reverse)

    grp = GDN_BATCH_GROUP if bsz % GDN_BATCH_GROUP == 0 else 1
    tok_qkv = pl.BlockSpec((grp, TOK_BLK, w3), lambda b, i: (b, blk(i), 0))
    tok_o = pl.BlockSpec((grp, TOK_BLK, GDN_WIDTH), lambda b, i: (b, blk(i), 0))
    gate_specs = [pl.BlockSpec((grp, TOK_BLK, LANES), lambda b, i: (b, blk(i), 0)),
                  pl.BlockSpec((grp, 16, TOK_BLK), lambda b, i: (b, 0, blk(i)))]
    par_specs = [pl.BlockSpec(gp.shape, lambda b, i: (0, 0)),
                 pl.BlockSpec(gpt.shape, lambda b, i: (0, 0, 0))]
    state = pltpu.VMEM((grp, GDN_HEADS, GDN_HEAD_DIM, GDN_HEAD_DIM), F32)
    o_shape = jax.ShapeDtypeStruct((bsz, s, GDN_WIDTH), BF16)
    body = functools.partial(_gdn_kernel, reverse=reverse, n_blk=nb, n_ctx_blk=n_ctx_blk)
    if reverse:
        return pl.pallas_call(
            body, grid=(bsz // grp, nb),
            in_specs=[tok_qkv] + gate_specs + par_specs,
            out_specs=tok_o, out_shape=o_shape, scratch_shapes=[state],
            compiler_params=_cparams(("parallel", "arbitrary")), name="gdn_bwd",
        )(qkv, gates, gates_t, gp, gpt)
    halo_specs = [
        pl.BlockSpec((grp, HALO_ROWS, w3),
                     lambda b, i: (b, jnp.maximum(blk(i) * halo_per_blk - 1, 0), 0)),
        pl.BlockSpec((grp, HALO_ROWS, w3),
                     lambda b, i: (b, jnp.minimum((blk(i) + 1) * halo_per_blk, n_halo - 1), 0))]
    return pl.pallas_call(
        body, grid=(bsz // grp, nb),
        in_specs=[tok_qkv] + halo_specs + gate_specs + [pl.BlockSpec(conv_w.shape, lambda b, i: (0, 0))]
        + par_specs,
        out_specs=[tok_o, tok_qkv],
        out_shape=[o_shape, jax.ShapeDtypeStruct((bsz, s, w3), BF16)],
        scratch_shapes=[pltpu.VMEM((grp, TOK_BLK + 16, w3), F32), state],
        compiler_params=_cparams(("parallel", "arbitrary")), name="gdn_fwd",
    )(qkv, qkv, qkv, gates, gates_t, conv_w, gp, gpt)


def _merge_chain(bb, refs, alpha, n_experts):
    (x_ref, ug_ref, vg_ref, of_ref, ob_ref, zs_ref, sga_ref, sgb_ref, mods_ref,
     slg_ref, slb_ref, sw_ref, sbias_ref, nw_ref, wpa_ref, wpb_ref, wo_ref,
     l1g_ref, l1b_ref) = refs[:19]
    if n_experts:
        router_ref, x1_ref, h2_ref, rcol_ref, rrow_ref, cnt_ref = refs[19:]
    else:
        x1_ref, h2_ref = refs[19:]
    tb = TOK_BLK
    vn = (_norm(vg_ref[bb].astype(F32)) * slg_ref[...] + slb_ref[...]).astype(BF16)
    yield
    gd = vn.shape[1] // SGU_GROUPS
    rows = []
    for c in range(tb // SGU_CHUNK):
        cols = []
        for g in range(SGU_GROUPS):
            cols.append(_dot(sw_ref[g], vn[c * SGU_CHUNK:(c + 1) * SGU_CHUNK, g * gd:(g + 1) * gd]))
        rows.append(jnp.concatenate(cols, axis=1) + sbias_ref[...])
    mixed = jnp.concatenate(rows, axis=0)
    yield
    oa = (ug_ref[bb].astype(F32) * mixed).astype(BF16)
    osum = of_ref[bb].astype(F32) + ob_ref[bb].astype(F32)
    heads = []
    for h in range(GDN_HEADS):
        oh = osum[:, h * GDN_HEAD_DIM:(h + 1) * GDN_HEAD_DIM]
        heads.append(oh * lax.rsqrt(jnp.mean(oh * oh, axis=-1, keepdims=True) + LN_EPS) * nw_ref[...])
    ogd = (jnp.concatenate(heads, axis=1) * zs_ref[bb].astype(F32)).astype(BF16)
    yield
    y = (sga_ref[bb].astype(F32) * _dot(oa, wpa_ref[...])
         + sgb_ref[bb].astype(F32) * _dot(ogd, wpb_ref[...]))
    yield
    t = _dot(y.astype(BF16), wo_ref[...])
    yield
    x1 = _norm(alpha * x_ref[bb] + mods_ref[bb, 0, 2:3, :] * t) * l1g_ref[...] + l1b_ref[...]
    x1_ref[bb] = x1
    h2 = _norm(x1) * (1.0 + mods_ref[bb, 0, 4:5, :]) + mods_ref[bb, 0, 3:4, :]
    h2_ref[bb] = h2.astype(BF16)
    yield
    if n_experts:
        logits = _dot3(h2, router_ref[...])
        yield
        lane = lax.broadcasted_iota(jnp.int32, logits.shape, 1).astype(F32)
        lg = jnp.where(lane < n_experts, logits, -jnp.inf)
        m1 = jnp.max(lg, axis=-1, keepdims=True)
        i1 = jnp.min(jnp.where(lg == m1, lane, float(LANES)), axis=-1, keepdims=True)
        lg2 = jnp.where(lane == i1, -jnp.inf, lg)
        m2 = jnp.max(lg2, axis=-1, keepdims=True)
        i2 = jnp.min(jnp.where(lg2 == m2, lane, float(LANES)), axis=-1, keepdims=True)
        e2 = jnp.exp(m2 - m1)
        den = 1.0 + e2
        gate = jnp.where(lane == i1, 1.0 / den, 0.0) + jnp.where(lane == i2, e2 / den, 0.0)
        yield
        sel = jnp.where(jnp.logical_or(lane == i1, lane == i2), 1.0, 0.0)
        selb = sel.astype(BF16)
        ti = lax.broadcasted_iota(jnp.int32, (tb, tb), 0)
        tj = lax.broadcasted_iota(jnp.int32, (tb, tb), 1)
        before = jnp.where(tj < ti, 1.0, 0.0).astype(BF16)
        rcol_ref[bb] = jnp.where(sel > 0.0, _dot(before, selb), MOE_NOT_SELECTED)
        er = lax.broadcasted_iota(jnp.int32, (8, LANES), 0)
        ec = lax.broadcasted_iota(jnp.int32, (8, LANES), 1)
        eye8 = jnp.where(er == ec, 1.0, 0.0).astype(BF16)
        sel_t = _dot_nt(eye8, selb)
        after = jnp.where(ti < tj, 1.0, 0.0).astype(BF16)
        rrow_ref[bb, 0, 0:8, :] = jnp.where(sel_t > 0.0, _dot(sel_t.astype(BF16), after), MOE_NOT_SELECTED)
        g_hi = gate.astype(BF16)
        g_mid = (gate - g_hi.astype(F32)).astype(BF16)
        g_lo = (gate - g_hi.astype(F32) - g_mid.astype(F32)).astype(BF16)
        rrow_ref[bb, 0, 8:16, :] = _dot_nt(eye8, g_hi) + (_dot_nt(eye8, g_mid) + _dot_nt(eye8, g_lo))
        cnt_ref[bb, 0] = _dot(jnp.ones((8, tb), BF16), selb)


def _merge_kernel(*refs, alpha, n_experts):
    chains = [_merge_chain(bb, refs, alpha, n_experts) for bb in range(refs[0].shape[0])]
    while chains:
        chains = [c for c in chains if next(c, chains) is not chains]


def _merge(x, ug, vg, o_f, o_b, zs, sga, sgb, mods, sgu_ln_g, sgu_ln_b, sgu_w, sgu_bias, norm_w,
           w_pa, w_pb, w_o, ln_g, ln_b, n_ctx_blk, alpha, router=None):
    bsz, s, d = x.shape
    nb = s // TOK_BLK
    sw = d // 2
    n_experts = 0 if router is None else router[1]
    grp = MERGE_BATCH_GROUP if bsz % MERGE_BATCH_GROUP == 0 else 1

    def tok(width):
        return pl.BlockSpec((grp, TOK_BLK, width), lambda b, i: (b, i, 0))

    def full(a):
        return pl.BlockSpec(a.shape, lambda b, i: (0,) * a.ndim)

    consts = [sgu_ln_g.reshape(1, sw), sgu_ln_b.reshape(1, sw), sgu_w, sgu_bias,
              norm_w.reshape(1, GDN_HEAD_DIM), w_pa, w_pb, w_o, ln_g.reshape(1, d), ln_b.reshape(1, d)]
    if n_experts:
        consts.append(router[0])
    in_specs = [tok(d), tok(sw), tok(sw), tok(GDN_WIDTH), tok(GDN_WIDTH), tok(GDN_WIDTH), tok(d), tok(d),
                pl.BlockSpec((grp, 1, 6, d), lambda b, i: (b, jnp.where(i < n_ctx_blk, 0, 1), 0, 0))]
    in_specs += [full(a) for a in consts]
    out_specs = [tok(d), tok(d)]
    out_shape = [jax.ShapeDtypeStruct((bsz, s, d), F32), jax.ShapeDtypeStruct((bsz, s, d), BF16)]
    if n_experts:
        assert n_experts <= 8
        out_specs += [tok(LANES),
                      pl.BlockSpec((grp, 1, 16, TOK_BLK), lambda b, i: (b, i, 0, 0)),
                      pl.BlockSpec((grp, 1, 8, LANES), lambda b, i: (b, i, 0, 0))]
        out_shape += [jax.ShapeDtypeStruct((bsz, s, LANES), F32),
                      jax.ShapeDtypeStruct((bsz, nb, 16, TOK_BLK), F32),
                      jax.ShapeDtypeStruct((bsz, nb, 8, LANES), F32)]
    return pl.pallas_call(
        functools.partial(_merge_kernel, alpha=alpha, n_experts=n_experts),
        grid=(bsz // grp, nb),
        in_specs=in_specs,
        out_specs=out_specs,
        out_shape=out_shape,
        compiler_params=_cparams(("parallel", "arbitrary")),
        name="merge",
    )(x, ug, vg, o_f, o_b, zs, sga, sgb, mods, *consts)


def _ffn_kernel(h_ref, w1_ref, w3_ref, w2_ref, o_ref, acc_ref):
    f = pl.program_id(1)

    @pl.when(f == 0)
    def _():
        acc_ref[...] = jnp.zeros_like(acc_ref)

    h = h_ref[...]
    mid = (_silu(_dot(h, w1_ref[...].astype(BF16))) * _dot(h, w3_ref[...].astype(BF16))).astype(BF16)
    acc_ref[...] += _dot(mid, w2_ref[...].astype(BF16))

    @pl.when(f == pl.num_programs(1) - 1)
    def _():
        o_ref[...] = acc_ref[...].astype(o_ref.dtype)


def _ffn(h, w1, w3, w2, tm, tf):
    t, d = h.shape
    ff = w1.shape[1]
    return pl.pallas_call(
        _ffn_kernel,
        grid=(t // tm, ff // tf),
        in_specs=[pl.BlockSpec((tm, d), lambda i, f: (i, 0)),
                  pl.BlockSpec((d, tf), lambda i, f: (0, f)),
                  pl.BlockSpec((d, tf), lambda i, f: (0, f)),
                  pl.BlockSpec((tf, d), lambda i, f: (f, 0))],
        out_specs=pl.BlockSpec((tm, d), lambda i, f: (i, 0)),
        out_shape=jax.ShapeDtypeStruct((t, d), BF16),
        scratch_shapes=[pltpu.VMEM((tm, d), F32)],
        compiler_params=_cparams(("parallel", "arbitrary")),
        name="ffn",
    )(h, w1, w3, w2)


MOE_GATHER_ROWS = 128
MOE_FFN_ROWS = 512
MOE_ROW_ALIGN = 16
MOE_NOT_SELECTED = -4096.0
MOE_TOKEN_TILE = 3072
FFN_TOKEN_TILE = 1536


def _moe_rows_capacity(tm):
    rows = tm + TOK_BLK + LANES
    return -(-rows // MOE_FFN_ROWS) * MOE_FFN_ROWS


def _moe_kernel(cnt_ref, off_ref, tot_ref, h_ref, rcol_ref, rrow_ref, w1_ref, w3_ref, w2_ref,
                o_ref, xs_ref, ys_ref, wrow_ref, *, n_experts):
    i = pl.program_id(0)
    e = pl.program_id(1)
    f = pl.program_id(2)
    tm = h_ref.shape[0]
    n_sub = tm // TOK_BLK
    gr = MOE_GATHER_ROWS
    fr = MOE_FFN_ROWS
    tot = tot_ref[i * n_experts + e]
    quarter = fr // 4
    n_quarters = _blk_id(tot + (quarter - 1), quarter)
    n_full = _blk_id(n_quarters, 4)
    has_half = jnp.bitwise_and(n_quarters, 2) != 0
    has_quarter = jnp.bitwise_and(n_quarters, 1) != 0

    def piece(j):
        base = (i * n_sub + j) * n_experts + e
        off = off_ref[base]
        start = pl.multiple_of(_blk_id(off, MOE_ROW_ALIGN) * MOE_ROW_ALIGN, MOE_ROW_ALIGN)
        return start, off - start, cnt_ref[base]

    @pl.when(jnp.logical_and(e == 0, f == 0))
    def _():
        o_ref[...] = jnp.zeros_like(o_ref)
        xs_ref[...] = jnp.zeros_like(xs_ref)
        ys_ref[...] = jnp.zeros_like(ys_ref)
        wrow_ref[...] = jnp.zeros_like(wrow_ref)

    @pl.when(f == 0)
    def _():
        def clear(rb, carry):
            rows = pl.ds(pl.multiple_of(rb * fr, fr), fr)
            xs_ref[rows, :] = jnp.zeros((fr, xs_ref.shape[1]), BF16)
            ys_ref[rows, :] = jnp.zeros((fr, ys_ref.shape[1]), F32)
            wrow_ref[rows, :] = jnp.zeros((fr, LANES), F32)
            return carry

        lax.fori_loop(0, _blk_id(tot + (fr - 1), fr) + 1, clear, 0)

        row_id = lax.broadcasted_iota(jnp.int32, (gr, TOK_BLK), 0).astype(F32)
        for j in range(n_sub):
            start, shift, cnt = piece(j)
            rank = rrow_ref[j, pl.ds(e, 1), :]
            gate = rrow_ref[j, pl.ds(8 + e, 1), :]
            row0 = row_id - shift.astype(F32)

            def gather(b, carry, j=j, start=start, rank=rank, gate=gate, row0=row0):
                hit = rank == row0 + (b * gr).astype(F32)
                rows = _dot(jnp.where(hit, 1.0, 0.0).astype(BF16), h_ref[j * TOK_BLK:(j + 1) * TOK_BLK, :])
                dst = pl.ds(pl.multiple_of(start + b * gr, MOE_ROW_ALIGN), gr)
                xs_ref[dst, :] = xs_ref[dst, :] + rows.astype(BF16)
                wgt = jnp.sum(jnp.where(hit, gate, 0.0), axis=-1, keepdims=True)
                wrow_ref[dst, :] = wrow_ref[dst, :] + jnp.broadcast_to(wgt, (gr, LANES))
                return carry

            lax.fori_loop(0, _blk_id(shift + cnt + (gr - 1), gr), gather, 0)

    def ffn_rows(r0, n):
        x = xs_ref[pl.ds(r0, n), :]
        mid = _silu(_dot(x, w1_ref[0])) * _dot(x, w3_ref[0]) * wrow_ref[pl.ds(r0, n), :][:, 0:1]
        ys_ref[pl.ds(r0, n), :] += _dot(mid.astype(BF16), w2_ref[0].astype(BF16))

    def ffn(rb, carry):
        ffn_rows(pl.multiple_of(rb * fr, fr), fr)
        return carry

    lax.fori_loop(0, n_full, ffn, 0)

    @pl.when(has_half)
    def _():
        ffn_rows(pl.multiple_of(n_full * fr, fr), 2 * quarter)

    @pl.when(has_quarter)
    def _():
        r0 = n_full * fr + jnp.where(has_half, 2 * quarter, 0)
        ffn_rows(pl.multiple_of(r0, quarter), quarter)

    @pl.when(f == pl.num_programs(2) - 1)
    def _():
        col_id = lax.broadcasted_iota(jnp.int32, (TOK_BLK, LANES), 1).astype(F32)
        lane = lax.broadcasted_iota(jnp.int32, (TOK_BLK, LANES), 1)
        for j in range(n_sub):
            start, shift, cnt = piece(j)
            tok = slice(j * TOK_BLK, (j + 1) * TOK_BLK)

            def rank_col(tok=tok):
                return jnp.sum(jnp.where(lane == e, rcol_ref[tok, :], 0.0), axis=-1, keepdims=True)

            for w in range(TOK_BLK // LANES + 1):

                @pl.when(jnp.logical_and(cnt > 0, shift + cnt > w * LANES))
                def _(w=w, tok=tok, start=start, shift=shift, rank_col=rank_col):
                    cols = col_id - (shift - w * LANES).astype(F32)
                    onehot = jnp.where(rank_col() == cols, 1.0, 0.0).astype(BF16)
                    rows = pl.ds(pl.multiple_of(start + w * LANES, MOE_ROW_ALIGN), LANES)
                    o_ref[tok, :] += _dot(onehot, ys_ref[rows, :].astype(BF16))


def _moe(h, rcol, rrow, cnt, off, tot, w1, w3, w2, tm, tf):
    t, d = h.shape
    ne, _, ff = w1.shape
    n_sub = tm // TOK_BLK
    cap = _moe_rows_capacity(tm)
    once = pl.Buffered(1)
    grid_spec = pltpu.PrefetchScalarGridSpec(
        num_scalar_prefetch=3,
        grid=(t // tm, ne, ff // tf),
        in_specs=[pl.BlockSpec((tm, d), lambda i, e, f, *_: (i, 0), pipeline_mode=once),
                  pl.BlockSpec((tm, LANES), lambda i, e, f, *_: (i, 0), pipeline_mode=once),
                  pl.BlockSpec((n_sub, 16, TOK_BLK), lambda i, e, f, *_: (i, 0, 0)),
                  pl.BlockSpec((1, d, tf), lambda i, e, f, *_: (e, 0, f)),
                  pl.BlockSpec((1, d, tf), lambda i, e, f, *_: (e, 0, f)),
                  pl.BlockSpec((1, tf, d), lambda i, e, f, *_: (e, f, 0))],
        out_specs=pl.BlockSpec((tm, d), lambda i, e, f, *_: (i, 0), pipeline_mode=once),
        scratch_shapes=[pltpu.VMEM((cap, d), BF16), pltpu.VMEM((cap, d), F32),
                        pltpu.VMEM((cap, LANES), F32)],
    )
    return pl.pallas_call(
        functools.partial(_moe_kernel, n_experts=ne),
        grid_spec=grid_spec,
        out_shape=jax.ShapeDtypeStruct((t, d), F32),
        compiler_params=_cparams(("parallel", "arbitrary", "arbitrary")),
        name="moe",
    )(cnt, off, tot, h, rcol, rrow, w1, w3, w2)


def _final_kernel(x_ref, f_ref, mods_ref, g_ref, b_ref, o_ref, *, alpha):
    x = alpha * x_ref[0] + mods_ref[0, 0, 5:6, :] * f_ref[0].astype(F32)
    o_ref[0] = _norm(x) * g_ref[...] + b_ref[...]


def _final(x1, f, mods, ln_g, ln_b, n_ctx_blk, alpha):
    bsz, s, d = x1.shape
    nb = s // TOK_BLK - n_ctx_blk
    return pl.pallas_call(
        functools.partial(_final_kernel, alpha=alpha),
        grid=(bsz, nb),
        in_specs=[pl.BlockSpec((1, TOK_BLK, d), lambda b, i: (b, i + n_ctx_blk, 0)),
                  pl.BlockSpec((1, TOK_BLK, d), lambda b, i: (b, i + n_ctx_blk, 0)),
                  pl.BlockSpec((1, 1, 6, d), lambda b, i: (b, 1, 0, 0)),
                  pl.BlockSpec((1, d), lambda b, i: (0, 0)),
                  pl.BlockSpec((1, d), lambda b, i: (0, 0))],
        out_specs=pl.BlockSpec((1, TOK_BLK, d), lambda b, i: (b, i, 0)),
        out_shape=jax.ShapeDtypeStruct((bsz, nb * TOK_BLK, d), F32),
        compiler_params=_cparams(("parallel", "arbitrary")),
        name="final",
    )(x1, f, mods, ln_g.reshape(1, d), ln_b.reshape(1, d))


def _token_tile(t, target):
    best = TOK_BLK
    for m in range(1, t // TOK_BLK + 1):
        cand = m * TOK_BLK
        if cand <= target and t % cand == 0:
            best = cand
    return best


def _ff_tile(ff, target):
    best = LANES
    for m in range(1, ff // LANES + 1):
        cand = m * LANES
        if cand <= target and ff % cand == 0:
            best = cand
    return best


def kernel(x, c, ctx, c_ctx, w_ada, b_ada, w_in, sgu_ln_g, sgu_ln_b, sgu_w, sgu_b, conv_w, a_log, dt_bias, gdn_norm_w, w_pa, w_pb, w_o, ln1_g, ln1_b, ln2_g, ln2_b, ffn_w1, ffn_w3, ffn_w2, moe_router, moe_w1, moe_w3, moe_w2):
    bsz, seq, d = x.shape
    ctx_len = ctx.shape[1]
    depth = w_ada.shape[0]
    alpha = (2.0 * depth) ** 0.25
    s = ctx_len + seq
    assert ctx_len % TOK_BLK == 0 and seq % TOK_BLK == 0
    assert d // 2 == SGU_GROUPS * LANES and bsz < 8
    n_ctx_blk = ctx_len // TOK_BLK
    sw = d // 2
    gw = GDN_WIDTH
    nh = GDN_HEADS

    cond = jnp.zeros((8, d), F32).at[:bsz].set(c).at[bsz].set(c_ctx)

    t_all = bsz * s
    tm_ffn = _token_tile(t_all, FFN_TOKEN_TILE)

    f_prev = None
    prev = None
    x_cur = x
    for l in range(depth):
        mod = _ada(cond, w_ada[l], b_ada[l]).reshape(8, 6, d)
        mods = jnp.stack([jnp.broadcast_to(mod[bsz], (bsz, 6, d)), mod[:bsz]], axis=1)

        g0 = 2 * sw + 4 * gw
        w_main = jnp.concatenate([w_in[l][:, :g0], w_in[l][:, g0 + 4 * nh:]], axis=1).astype(BF16)
        w_gate = jnp.pad(w_in[l][:, g0:g0 + 4 * nh], ((0, 0), (0, LANES - 4 * nh))).astype(BF16)
        w_gate_t = w_in[l][:, g0:g0 + 4 * nh].T.astype(BF16)
        outs = _inproj(x_cur, w_main, w_gate, w_gate_t, mods, n_ctx_blk, prev=prev,
                       ctx=ctx if prev is None else None, alpha=alpha)
        x_cur, ug, vg, qkv, zs, sga, sgb, gates, gates_t = outs

        al = jnp.zeros((LANES,), F32)
        db = jnp.zeros((LANES,), F32)
        for dd in range(2):
            al = al.at[2 * nh * dd:2 * nh * dd + nh].set(a_log[l, dd])
            db = db.at[2 * nh * dd:2 * nh * dd + nh].set(dt_bias[l, dd])
        gp = jnp.zeros((8, LANES), F32).at[0].set(al).at[1].set(db)
        gpt = jnp.stack([jnp.broadcast_to(al[:16, None], (16, LANES)),
                         jnp.broadcast_to(db[:16, None], (16, LANES))])
        cw = jnp.pad(conv_w[l], ((0, 8 - CONV_K), (0, 0)))
        o_f, qkvn = _gdn(qkv, gates, gates_t, cw, gp, gpt, n_ctx_blk, reverse=False)
        o_b = _gdn(qkvn, gates, gates_t, None, gp, gpt, n_ctx_blk, reverse=True)

        sgu_bias = jnp.repeat(sgu_b[l].T, LANES, axis=1)
        is_moe = l % 2 == 1
        router = None
        if is_moe:
            r = moe_router[l // 2]
            router = (jnp.pad(r, ((0, 0), (0, LANES - r.shape[1]))), r.shape[1])
        outs = _merge(x_cur, ug, vg, o_f, o_b, zs, sga, sgb, mods, sgu_ln_g[l], sgu_ln_b[l],
                      sgu_w[l].astype(BF16), sgu_bias, gdn_norm_w[l], w_pa[l].astype(BF16),
                      w_pb[l].astype(BF16), w_o[l].astype(BF16), ln1_g[l], ln1_b[l], n_ctx_blk, alpha,
                      router=router)
        x1, h2 = outs[0], outs[1]

        h2f = h2.reshape(t_all, d)
        if is_moe:
            ne = moe_router.shape[-1]
            rcol = outs[2].reshape(t_all, LANES)
            rrow = outs[3].reshape(t_all // TOK_BLK, 16, TOK_BLK)
            tm = _token_tile(t_all, MOE_TOKEN_TILE)
            n_sub = tm // TOK_BLK
            cnt = outs[4][:, :, 0, :ne].astype(jnp.int32).reshape(t_all // tm, n_sub, ne)
            off = jnp.cumsum(cnt, axis=1) - cnt
            tot = jnp.sum(cnt, axis=1)
            tf = _ff_tile(moe_w1.shape[-1], 512)
            f = _moe(h2f, rcol, rrow, cnt.reshape(-1), off.reshape(-1), tot.reshape(-1),
                     moe_w1[l // 2].astype(BF16), moe_w3[l // 2].astype(BF16),
                     moe_w2[l // 2], tm, tf)
        else:
            tf = _ff_tile(ffn_w1.shape[-1], 512)
            f = _ffn(h2f, ffn_w1[l // 2], ffn_w3[l // 2], ffn_w2[l // 2], tm_ffn, tf)
        f = f.reshape(bsz, s, d)
        prev = (f, mods, ln2_g[l], ln2_b[l])
        f_prev, mods_prev = f, mods
        x_cur = x1

    return _final(x_cur, f_prev, mods_prev, ln2_g[depth - 1], ln2_b[depth - 1], n_ctx_blk, alpha)
```
